```python
import math
import jax, jax.numpy as jnp
from jax import lax
import numpy as np

D_MODEL = 4096
BATCH = 4
SEQ = 4096
DEPTH = 4

N_MIXERS = 2
N_POOL_LAYERS = (DEPTH + 1) // 2
N_ATTN_LAYERS = DEPTH // 2
POOL_WINDOWS = (2, 4, 8, 16)
N_POOL_GROUPS = len(POOL_WINDOWS)
POOL_GROUP = D_MODEL // N_POOL_GROUPS
HEAD_DIM = 128
N_HEADS = D_MODEL // HEAD_DIM
BLOCK_Q = 128
D_FF = 4 * D_MODEL
PLE_DIM = 256
RMS_EPS = 1e-6

kernel_name = "hybrid_pool_stickbreak_trunk"


def rmsnorm(h, g):
    hf = h.astype(jnp.float32)
    y = hf * lax.rsqrt(jnp.mean(hf * hf, axis=-1, keepdims=True) + RMS_EPS)
    return (y * g.astype(jnp.float32)).astype(h.dtype)


def pool_mixer(h, w_pool, scale):
    B, S, D = h.shape
    hg = h.reshape(B, S, N_POOL_GROUPS, POOL_GROUP)
    pos = jnp.arange(S)
    outs = []
    for g, w in enumerate(POOL_WINDOWS):
        xg = hg[:, :, g].astype(jnp.float32)
        c = jnp.cumsum(xg, axis=1)
        c_lag = jnp.pad(c[:, :S - w], ((0, 0), (w, 0), (0, 0)))
        cnt = jnp.minimum(pos + 1, w).astype(jnp.float32)[None, :, None]
        pooled = (c - c_lag) / cnt - xg
        outs.append(jnp.einsum('bsc,cd->bsd', pooled.astype(h.dtype), w_pool[g]))
    return jnp.concatenate(outs, axis=-1) * scale


def stick_breaking_attention(h, w_qkv, w_o):
    B, S, D = h.shape
    qkv = h @ w_qkv
    q, k, v = jnp.split(qkv, 3, axis=-1)
    to_heads = lambda t: t.reshape(B, S, N_HEADS, HEAD_DIM).transpose(0, 2, 1, 3)
    q, k, v = to_heads(q), to_heads(k), to_heads(v)
    inv_sqrt_d = 1.0 / math.sqrt(HEAD_DIM)
    outs = []
    for blk in range(S // BLOCK_Q):
        q0 = blk * BLOCK_Q
        q1 = q0 + BLOCK_Q
        qb = q[:, :, q0:q1]
        kb = k[:, :, :q1]
        vb = v[:, :, :q1]
        z = jnp.einsum('bhqd,bhkd->bhqk', qb, kb,
                       preferred_element_type=jnp.float32) * inv_sqrt_d
        t_idx = (q0 + jnp.arange(BLOCK_Q))[:, None]
        s_idx = jnp.arange(q1)[None, :]
        mask = s_idx < t_idx
        log_beta = jax.nn.log_sigmoid(z)
        log_1m_beta = jnp.where(mask, jax.nn.log_sigmoid(-z), 0.0)
        log_rest = lax.cumsum(log_1m_beta, axis=3, reverse=True) - log_1m_beta
        attn = jnp.where(mask, jnp.exp(log_beta + log_rest), 0.0)
        outs.append(jnp.einsum('bhqk,bhkd->bhqd', attn.astype(vb.dtype), vb))
    o = jnp.concatenate(outs, axis=2)
    o = o.transpose(0, 2, 1, 3).reshape(B, S, D)
    return o @ w_o


def sq_relu_mlp(h, w_up, w_down):
    a = jax.nn.relu(h @ w_up)
    return (a * a) @ w_down


def setup_inputs(seed: int = 0) -> dict:
    key = jax.random.key(seed)
    ks = jax.random.split(key, 16)
    f32 = jnp.float32
    nrm = lambda k, shape, fan_in: jax.random.normal(k, shape, f32) * (fan_in ** -0.5)
    gain = lambda k, shape: 1.0 + 0.05 * jax.random.normal(k, shape, f32)
    D = D_MODEL
    return {
        "x": jax.random.normal(ks[0], (BATCH, SEQ, D), f32),
        "p": jax.random.normal(ks[1], (DEPTH, BATCH, SEQ, PLE_DIM), f32),
        "g_mix": gain(ks[2], (DEPTH, D)),
        "g_mlp": gain(ks[3], (DEPTH, D)),
        "w_pool": nrm(ks[4], (N_POOL_LAYERS, N_POOL_GROUPS, POOL_GROUP, POOL_GROUP), POOL_GROUP),
        "pool_scale": 1.0 + 0.1 * jax.random.normal(ks[5], (N_POOL_LAYERS, D), f32),
        "w_qkv": nrm(ks[6], (N_ATTN_LAYERS, D, 3 * D), D),
        "w_o": nrm(ks[7], (N_ATTN_LAYERS, D, D), D),
        "w_up": nrm(ks[8], (DEPTH, D, D_FF), D),
        "w_down": nrm(ks[9], (DEPTH, D_FF, D), D_FF),
        "w_pe": nrm(ks[10], (DEPTH, PLE_DIM, D), PLE_DIM),
        "g_pe": gain(ks[11], (DEPTH, D)),
        "g_gate": gain(ks[12], (DEPTH, D)),
        "w_gate": nrm(ks[13], (DEPTH, D, D), D),
        "g_final": gain(ks[14], (D,)),
    }


def reference(x, p, g_mix, g_mlp, w_pool, pool_scale, w_qkv, w_o, w_up, w_down,
              w_pe, g_pe, g_gate, w_gate, g_final):
    h = x
    for i in range(DEPTH):
        hn = rmsnorm(h, g_mix[i])
        if i % N_MIXERS == 0:
            h = h + pool_mixer(hn, w_pool[i // N_MIXERS], pool_scale[i // N_MIXERS])
        else:
            h = h + stick_breaking_attention(hn, w_qkv[i // N_MIXERS], w_o[i // N_MIXERS])
        h = h + sq_relu_mlp(rmsnorm(h, g_mlp[i]), w_up[i], w_down[i])
        e = rmsnorm(p[i] @ w_pe[i], g_pe[i])
        gate = jax.nn.sigmoid(rmsnorm(h, g_gate[i]) @ w_gate[i])
        h = h + gate * e
    return rmsnorm(h, g_final)
```

```python
import functools
import math

import jax
import jax.numpy as jnp
from jax import lax
from jax.experimental import pallas as pl
from jax.experimental.pallas import tpu as pltpu

HEAD_DIM = 128
POOL_WINDOWS = (2, 4, 8, 16)
POOL_HALO = 16
RMS_EPS = 1e-6

V7X_VMEM_BYTES = 64 * 1024 * 1024
VMEM_LIMIT_BYTES = V7X_VMEM_BYTES - 8 * 1024 * 1024

_BF16 = jnp.bfloat16
_F32 = jnp.float32


def _tile(dim, pref):
    t = min(dim, pref)
    while dim % t:
        t //= 2
    return t


def _params(semantics):
    return pltpu.CompilerParams(dimension_semantics=semantics, vmem_limit_bytes=VMEM_LIMIT_BYTES)


def _rms(x, g):
    ms = jnp.mean(x * x, axis=-1, keepdims=True)
    return x * lax.rsqrt(ms + RMS_EPS) * g


def _rmsnorm_kernel(x_ref, g_ref, o_ref):
    o_ref[...] = _rms(x_ref[...], g_ref[...]).astype(o_ref.dtype)


def _rmsnorm(x, g, out_dtype):
    n, d = x.shape
    tm = _tile(n, 256)
    return pl.pallas_call(
        _rmsnorm_kernel,
        grid=(n // tm,),
        in_specs=[pl.BlockSpec((tm, d), lambda i: (i, 0)),
                  pl.BlockSpec((1, d), lambda i: (0, 0))],
        out_specs=pl.BlockSpec((tm, d), lambda i: (i, 0)),
        out_shape=jax.ShapeDtypeStruct((n, d), out_dtype),
        compiler_params=_params(("parallel",)),
        name="rmsnorm",
    )(x, g.reshape(1, d))


def _ep_relu2(acc):
    return jnp.square(jnp.maximum(acc, 0.0))


def _ep_identity(acc):
    return acc


def _ep_residual(acc, res):
    return res + acc


def _ep_gate(acc, res, e):
    return res + jax.nn.sigmoid(acc) * e


def _matmul_kernel(a_ref, w_ref, *rest, nk, n_extra, epilogue):
    extras = rest[:n_extra]
    o_ref = rest[n_extra]
    part = jnp.dot(a_ref[...], w_ref[...], preferred_element_type=_F32)

    def finish(acc):
        o_ref[...] = epilogue(acc, *[e[...] for e in extras]).astype(o_ref.dtype)

    if nk == 1:
        finish(part)
        return
    acc_ref = rest[n_extra + 1]
    k = pl.program_id(2)

    @pl.when(k == 0)
    def _():
        acc_ref[...] = part

    @pl.when(jnp.logical_and(k > 0, k < nk - 1))
    def _():
        acc_ref[...] += part

    @pl.when(k == nk - 1)
    def _():
        finish(acc_ref[...] + part)


def _matmul(a, w, extras, epilogue, out_dtype, *, tm, tn, tk, name):
    m, k = a.shape
    _, n = w.shape
    tm, tn, tk = _tile(m, tm), _tile(n, tn), _tile(k, tk)
    nk = k // tk
    tile_spec = pl.BlockSpec((tm, tn), lambda i, j, kk: (i, j))
    kernel = functools.partial(_matmul_kernel, nk=nk, n_extra=len(extras), epilogue=epilogue)
    return pl.pallas_call(
        kernel,
        grid=(m // tm, n // tn, nk),
        in_specs=[pl.BlockSpec((tm, tk), lambda i, j, kk: (i, kk)),
                  pl.BlockSpec((tk, tn), lambda i, j, kk: (kk, j))] + [tile_spec] * len(extras),
        out_specs=tile_spec,
        out_shape=jax.ShapeDtypeStruct((m, n), out_dtype),
        scratch_shapes=[pltpu.VMEM((tm, tn), _F32)] if nk > 1 else [],
        compiler_params=_params(("parallel", "parallel", "arbitrary")),
        name=name,
    )(a, w, *extras)


def _pool_kernel(h_ref, halo_ref, g_ref, w_ref, scale_ref, o_ref, xn_ref, *, ts, dg):
    i = pl.program_id(1)
    g = g_ref[...]
    h = h_ref[0]
    xn_ref[0:POOL_HALO, :] = jnp.where(i > 0, _rms(halo_ref[0], g), 0.0)
    xn_ref[POOL_HALO:, :] = _rms(h, g)
    pos = i * ts + lax.broadcasted_iota(jnp.int32, (ts, 1), 0)
    for gi, win in enumerate(POOL_WINDOWS):
        cols = slice(gi * dg, (gi + 1) * dg)
        cur = xn_ref[POOL_HALO:POOL_HALO + ts, cols]
        tot = cur
        for j in range(1, win):
            tot = tot + xn_ref[POOL_HALO - j:POOL_HALO - j + ts, cols]
        cnt = jnp.minimum(pos + 1, win).astype(_F32)
        pooled = tot / cnt - cur
        y = jnp.dot(pooled.astype(_BF16), w_ref[gi], preferred_element_type=_F32)
        o_ref[0, :, cols] = h[:, cols] + y * scale_ref[:, cols]


def _pool_layer(h, g, w_pool, scale):
    b, s, d = h.shape
    ng = len(POOL_WINDOWS)
    dg = d // ng
    ts = _tile(s, 256)
    halo_blocks = ts // POOL_HALO
    kernel = functools.partial(_pool_kernel, ts=ts, dg=dg)
    return pl.pallas_call(
        kernel,
        grid=(b, s // ts),
        in_specs=[
            pl.BlockSpec((1, ts, d), lambda bi, i: (bi, i, 0)),
            pl.BlockSpec((1, POOL_HALO, d),
                         lambda bi, i: (bi, jnp.maximum(i * halo_blocks - 1, 0), 0)),
            pl.BlockSpec((1, d), lambda bi, i: (0, 0)),
            pl.BlockSpec((ng, dg, dg), lambda bi, i: (0, 0, 0)),
            pl.BlockSpec((1, d), lambda bi, i: (0, 0)),
        ],
        out_specs=pl.BlockSpec((1, ts, d), lambda bi, i: (bi, i, 0)),
        out_shape=jax.ShapeDtypeStruct((b, s, d), _F32),
        scratch_shapes=[pltpu.VMEM((ts + POOL_HALO, d), _F32)],
        compiler_params=_params(("parallel", "parallel")),
        name="pool_mixer",
    )(h, h, g.reshape(1, d), w_pool, scale.reshape(1, d))


def _attn_kernel(q_ref, k_ref, v_ref, o_ref, *, tq, scale):
    qi = pl.program_id(2)
    q = q_ref[...]
    row = lax.broadcasted_iota(jnp.int32, (tq, tq), 0)
    col = lax.broadcasted_iota(jnp.int32, (tq, tq), 1)
    later_keys = jnp.where(row > col, 1.0, 0.0).astype(_BF16)
    causal = col < row

    def block(kb, vb, carry, acc, diagonal):
        z = lax.dot_general(q, kb, (((1,), (1,)), ((), ())), preferred_element_type=_F32) * scale
        sp = jnp.maximum(z, 0.0) + jnp.log(1.0 + jnp.exp(-jnp.abs(z)))
        log_beta = z - sp
        if diagonal:
            sp = jnp.where(causal, sp, 0.0)
        hi = sp.astype(_BF16)
        lo = (sp - hi.astype(_F32)).astype(_BF16)
        rest = (jnp.dot(hi, later_keys, preferred_element_type=_F32)
                + jnp.dot(lo, later_keys, preferred_element_type=_F32))
        attn = jnp.exp(log_beta - rest - carry)
        if diagonal:
            attn = jnp.where(causal, attn, 0.0)
        acc = acc + jnp.dot(attn.astype(_BF16), vb, preferred_element_type=_F32)
        carry = carry + jnp.sum(sp, axis=-1, keepdims=True)
        return carry, acc

    d0 = pl.multiple_of(qi * tq, tq)
    carry, acc = block(k_ref[pl.ds(d0, tq), :], v_ref[pl.ds(d0, tq), :],
                       jnp.zeros((tq, 1), _F32), jnp.zeros((tq, HEAD_DIM), _F32), True)

    def body(j, state):
        start = pl.multiple_of((qi - 1 - j) * tq, tq)
        return block(k_ref[pl.ds(start, tq), :], v_ref[pl.ds(start, tq), :], *state, False)

    carry, acc = lax.fori_loop(0, qi, body, (carry, acc))
    o_ref[...] = acc.astype(o_ref.dtype)


def _attention(qkv, b, s, d):
    nh = d // HEAD_DIM
    tq = _tile(s, 256)
    nq = s // tq
    kernel = functools.partial(_attn_kernel, tq=tq, scale=1.0 / math.sqrt(HEAD_DIM))
    return pl.pallas_call(
        kernel,
        grid=(b, nh, nq),
        in_specs=[
            pl.BlockSpec((tq, HEAD_DIM), lambda bi, hi, qi: (bi * nq + qi, hi)),
            pl.BlockSpec((s, HEAD_DIM), lambda bi, hi, qi: (bi, nh + hi)),
            pl.BlockSpec((s, HEAD_DIM), lambda bi, hi, qi: (bi, 2 * nh + hi)),
        ],
        out_specs=pl.BlockSpec((tq, HEAD_DIM), lambda bi, hi, qi: (bi * nq + qi, hi)),
        out_shape=jax.ShapeDtypeStruct((b * s, d), _BF16),
        compiler_params=_params(("parallel", "parallel", "arbitrary")),
        name="stickbreak_attn",
    )(qkv, qkv, qkv)


def _ple_kernel(p_ref, w_ref, g_ref, o_ref):
    y = jnp.dot(p_ref[0].astype(_BF16), w_ref[...], preferred_element_type=_F32)
    o_ref[...] = _rms(y, g_ref[...])


def _ple_embed(p, layer, w_pe, g_pe):
    _, n, pd = p.shape
    d = w_pe.shape[1]
    tm = _tile(n, 256)
    return pl.pallas_call(
        _ple_kernel,
        grid=(n // tm,),
        in_specs=[pl.BlockSpec((1, tm, pd), lambda i: (layer, i, 0)),
                  pl.BlockSpec((pd, d), lambda i: (0, 0)),
                  pl.BlockSpec((1, d), lambda i: (0, 0))],
        out_specs=pl.BlockSpec((tm, d), lambda i: (i, 0)),
        out_shape=jax.ShapeDtypeStruct((n, d), _F32),
        compiler_params=_params(("parallel",)),
        name="ple_embed",
    )(p, w_pe, g_pe.reshape(1, d))


def kernel(x, p, g_mix, g_mlp, w_pool, pool_scale, w_qkv, w_o, w_up, w_down,
           w_pe, g_pe, g_gate, w_gate, g_final):
    b, s, d = x.shape
    depth = g_mix.shape[0]
    n = b * s
    p2 = p.reshape(depth, n, p.shape[-1])
    bf = lambda w: w.astype(_BF16)

    h = x.reshape(n, d)
    for i in range(depth):
        if i % 2 == 0:
            h = _pool_layer(h.reshape(b, s, d), g_mix[i], bf(w_pool[i // 2]),
                            pool_scale[i // 2]).reshape(n, d)
        else:
            xn = _rmsnorm(h, g_mix[i], _BF16)
            qkv = _matmul(xn, bf(w_qkv[i // 2]), (), _ep_identity, _BF16,
                          tm=1024, tn=1024, tk=4096, name="qkv_proj")
            o = _attention(qkv, b, s, d)
            h = _matmul(o, bf(w_o[i // 2]), (h,), _ep_residual, _F32,
                        tm=1024, tn=512, tk=4096, name="attn_out_proj")
        xn = _rmsnorm(h, g_mlp[i], _BF16)
        a = _matmul(xn, bf(w_up[i]), (), _ep_relu2, _BF16,
                    tm=1024, tn=1024, tk=4096, name="mlp_up")
        h = _matmul(a, bf(w_down[i]), (h,), _ep_residual, _F32,
                    tm=1024, tn=1024, tk=2048, name="mlp_down")
        e = _ple_embed(p2, i, bf(w_pe[i]), g_pe[i])
        xg = _rmsnorm(h, g_gate[i], _BF16)
        h = _matmul(xg, bf(w_gate[i]), (h, e), _ep_gate, _F32,
                    tm=1024, tn=512, tk=4096, name="ple_gate")
    return _rmsnorm(h, g_final, _F32).reshape(b, s, d)
```

```python
import functools
import math

import jax
import jax.numpy as jnp
from jax import lax
from jax.experimental import pallas as pl
from jax.experimental.pallas import tpu as pltpu

HEAD_DIM = 128
POOL_WINDOWS = (2, 4, 8, 16)
POOL_HALO = 16
RMS_EPS = 1e-6
_LOG2_E = 1.0 / math.log(2.0)

V7X_VMEM_BYTES = 64 * 1024 * 1024
VMEM_LIMIT_BYTES = V7X_VMEM_BYTES - 8 * 1024 * 1024

_BF16 = jnp.bfloat16
_F32 = jnp.float32


def _tile(dim, pref):
    t = min(dim, pref)
    while dim % t:
        t //= 2
    return t


def _params(semantics):
    return pltpu.CompilerParams(dimension_semantics=semantics, vmem_limit_bytes=VMEM_LIMIT_BYTES)


def _rms(x, g):
    ms = jnp.mean(x * x, axis=-1, keepdims=True)
    return x * lax.rsqrt(ms + RMS_EPS) * g


def _rmsnorm_kernel(x_ref, g_ref, o_ref):
    o_ref[...] = _rms(x_ref[...], g_ref[...]).astype(o_ref.dtype)


def _rmsnorm(x, g, out_dtype):
    n, d = x.shape
    tm = _tile(n, 256)
    return pl.pallas_call(
        _rmsnorm_kernel,
        grid=(n // tm,),
        in_specs=[pl.BlockSpec((tm, d), lambda i: (i, 0)),
                  pl.BlockSpec((1, d), lambda i: (0, 0))],
        out_specs=pl.BlockSpec((tm, d), lambda i: (i, 0)),
        out_shape=jax.ShapeDtypeStruct((n, d), out_dtype),
        compiler_params=_params(("parallel",)),
        name="rmsnorm",
    )(x, g.reshape(1, d))


def _ep_relu2(acc):
    return jnp.square(jnp.maximum(acc, 0.0))


def _ep_identity(acc):
    return acc


def _ep_residual(acc, res):
    return res + acc


def _ep_gate(acc, res, e):
    return res + jax.nn.sigmoid(acc) * e


def _matmul_kernel(a_ref, w_ref, *rest, nk, n_extra, epilogue, on_residual):
    extras = rest[:n_extra]
    o_ref = rest[n_extra]

    def partial_product():
        return jnp.dot(a_ref[...], w_ref[...], preferred_element_type=_F32)

    def finish(acc):
        o_ref[...] = epilogue(acc, *[e[...] for e in extras]).astype(o_ref.dtype)

    if nk == 1:
        finish(partial_product())
        return
    k = pl.program_id(2)
    if on_residual:
        @pl.when(k == 0)
        def _():
            o_ref[...] = extras[0][...]

        o_ref[...] += partial_product()
        return
    acc_ref = rest[n_extra + 1]

    @pl.when(k == 0)
    def _():
        acc_ref[...] = jnp.zeros_like(acc_ref)

    acc_ref[...] += partial_product()

    @pl.when(k == nk - 1)
    def _():
        finish(acc_ref[...])


def _matmul(a, w, extras, epilogue, out_dtype, *, tm, tn, tk, name):
    m, k = a.shape
    _, n = w.shape
    tm, tn, tk = _tile(m, tm), _tile(n, tn), _tile(k, tk)
    nk = k // tk
    on_residual = nk > 1 and epilogue is _ep_residual and out_dtype == _F32
    tile_spec = pl.BlockSpec((tm, tn), lambda i, j, kk: (i, j))
    kernel = functools.partial(_matmul_kernel, nk=nk, n_extra=len(extras), epilogue=epilogue,
                               on_residual=on_residual)
    return pl.pallas_call(
        kernel,
        grid=(m // tm, n // tn, nk),
        in_specs=[pl.BlockSpec((tm, tk), lambda i, j, kk: (i, kk)),
                  pl.BlockSpec((tk, tn), lambda i, j, kk: (kk, j))] + [tile_spec] * len(extras),
        out_specs=tile_spec,
        out_shape=jax.ShapeDtypeStruct((m, n), out_dtype),
        scratch_shapes=[pltpu.VMEM((tm, tn), _F32)] if nk > 1 and not on_residual else [],
        compiler_params=_params(("parallel", "parallel", "arbitrary")),
        name=name,
    )(a, w, *extras)


def _pool_kernel(h_ref, halo_ref, g_ref, w_ref, scale_ref, o_ref, xn_ref, *, ts, dg):
    i = pl.program_id(1)
    g = g_ref[...]
    h = h_ref[0]
    xn_ref[0:POOL_HALO, :] = jnp.where(i > 0, _rms(halo_ref[0], g), 0.0)
    xn_ref[POOL_HALO:, :] = _rms(h, g)
    pos = i * ts + lax.broadcasted_iota(jnp.int32, (ts, 1), 0)
    for gi, win in enumerate(POOL_WINDOWS):
        cols = slice(gi * dg, (gi + 1) * dg)
        cur = xn_ref[POOL_HALO:POOL_HALO + ts, cols]
        tot = cur
        for j in range(1, win):
            tot = tot + xn_ref[POOL_HALO - j:POOL_HALO - j + ts, cols]
        cnt = jnp.minimum(pos + 1, win).astype(_F32)
        pooled = tot / cnt - cur
        y = jnp.dot(pooled.astype(_BF16), w_ref[gi], preferred_element_type=_F32)
        o_ref[0, :, cols] = h[:, cols] + y * scale_ref[:, cols]


def _pool_layer(h, g, w_pool, scale):
    b, s, d = h.shape
    ng = len(POOL_WINDOWS)
    dg = d // ng
    ts = _tile(s, 256)
    halo_blocks = ts // POOL_HALO
    kernel = functools.partial(_pool_kernel, ts=ts, dg=dg)
    return pl.pallas_call(
        kernel,
        grid=(b, s // ts),
        in_specs=[
            pl.BlockSpec((1, ts, d), lambda bi, i: (bi, i, 0)),
            pl.BlockSpec((1, POOL_HALO, d),
                         lambda bi, i: (bi, jnp.maximum(i * halo_blocks - 1, 0), 0)),
            pl.BlockSpec((1, d), lambda bi, i: (0, 0)),
            pl.BlockSpec((ng, dg, dg), lambda bi, i: (0, 0, 0)),
            pl.BlockSpec((1, d), lambda bi, i: (0, 0)),
        ],
        out_specs=pl.BlockSpec((1, ts, d), lambda bi, i: (bi, i, 0)),
        out_shape=jax.ShapeDtypeStruct((b, s, d), _F32),
        scratch_shapes=[pltpu.VMEM((ts + POOL_HALO, d), _F32)],
        compiler_params=_params(("parallel", "parallel")),
        name="pool_mixer",
    )(h, h, g.reshape(1, d), w_pool, scale.reshape(1, d))


ATTN_HEADS_PER_STEP = 8


def _attn_kernel(q_ref, k_ref, v_ref, o_ref, *, tq, nhb, log2_scale):
    qi = pl.program_id(2)
    row = lax.broadcasted_iota(jnp.int32, (tq, tq), 0)
    col = lax.broadcasted_iota(jnp.int32, (tq, tq), 1)
    from_key = jnp.where(row >= col, 1.0, 0.0).astype(_BF16)
    from_key2 = jnp.concatenate([from_key, from_key], axis=0)
    causal = col < row
    heads = [slice(hh * HEAD_DIM, (hh + 1) * HEAD_DIM) for hh in range(nhb)]
    sign_bit = jnp.uint32(0x80000000)

    def step(start, state, diagonal):
        zs = [lax.dot_general(q_ref[:, cols], k_ref[pl.ds(start, tq), cols],
                              (((1,), (1,)), ((), ())), preferred_element_type=_F32) * log2_scale
              for cols in heads]
        tail = []
        for z in zs:
            neg_abs = lax.bitcast_convert_type(lax.bitcast_convert_type(z, jnp.uint32) | sign_bit, _F32)
            sp = jnp.maximum(z, 0.0) + jnp.log(1.0 + jnp.exp2(neg_abs)) * _LOG2_E
            if diagonal:
                sp = jnp.where(causal, sp, 0.0)
            hi = sp.astype(_BF16)
            lo = (sp - hi.astype(_F32)).astype(_BF16)
            tail.append(jnp.dot(jnp.concatenate([hi, lo], axis=1), from_key2,
                                preferred_element_type=_F32))
        out = []
        for hh, cols in enumerate(heads):
            carry = state[2 * hh]
            attn = jnp.exp2(zs[hh] - carry - tail[hh])
            if diagonal:
                attn = jnp.where(causal, attn, 0.0)
            acc = state[2 * hh + 1] + jnp.dot(attn.astype(_BF16), v_ref[pl.ds(start, tq), cols],
                                              preferred_element_type=_F32)
            out.extend((carry + tail[hh][:, 0:1], acc))
        return tuple(out)

    init = (jnp.zeros((tq, 1), _F32), jnp.zeros((tq, HEAD_DIM), _F32)) * nhb
    state = step(pl.multiple_of(qi * tq, tq), init, True)

    def body(j, state):
        return step(pl.multiple_of((qi - 1 - j) * tq, tq), state, False)

    state = lax.fori_loop(0, qi, body, state)
    for hh, cols in enumerate(heads):
        o_ref[:, cols] = state[2 * hh + 1].astype(o_ref.dtype)


def _attention(qkv, b, s, d):
    nh = d // HEAD_DIM
    nhb = _tile(nh, ATTN_HEADS_PER_STEP)
    ng = nh // nhb
    tq = _tile(s, 256)
    nq = s // tq
    kernel = functools.partial(_attn_kernel, tq=tq, nhb=nhb,
                               log2_scale=_LOG2_E / math.sqrt(HEAD_DIM))
    wide = nhb * HEAD_DIM
    return pl.pallas_call(
        kernel,
        grid=(b, ng, nq),
        in_specs=[
            pl.BlockSpec((tq, wide), lambda bi, gi, qi: (bi * nq + qi, gi)),
            pl.BlockSpec((s, wide), lambda bi, gi, qi: (bi, ng + gi)),
            pl.BlockSpec((s, wide), lambda bi, gi, qi: (bi, 2 * ng + gi)),
        ],
        out_specs=pl.BlockSpec((tq, wide), lambda bi, gi, qi: (bi * nq + qi, gi)),
        out_shape=jax.ShapeDtypeStruct((b * s, d), _BF16),
        compiler_params=_params(("parallel", "parallel", "arbitrary")),
        name="stickbreak_attn",
    )(qkv, qkv, qkv)


def _ple_kernel(p_ref, w_ref, g_ref, o_ref):
    y = jnp.dot(p_ref[0].astype(_BF16), w_ref[...], preferred_element_type=_F32)
    o_ref[...] = _rms(y, g_ref[...])


def _ple_embed(p, layer, w_pe, g_pe):
    _, n, pd = p.shape
    d = w_pe.shape[1]
    tm = _tile(n, 256)
    return pl.pallas_call(
        _ple_kernel,
        grid=(n // tm,),
        in_specs=[pl.BlockSpec((1, tm, pd), lambda i: (layer, i, 0)),
                  pl.BlockSpec((pd, d), lambda i: (0, 0)),
                  pl.BlockSpec((1, d), lambda i: (0, 0))],
        out_specs=pl.BlockSpec((tm, d), lambda i: (i, 0)),
        out_shape=jax.ShapeDtypeStruct((n, d), _F32),
        compiler_params=_params(("parallel",)),
        name="ple_embed",
    )(p, w_pe, g_pe.reshape(1, d))


def kernel(x, p, g_mix, g_mlp, w_pool, pool_scale, w_qkv, w_o, w_up, w_down,
           w_pe, g_pe, g_gate, w_gate, g_final):
    b, s, d = x.shape
    depth = g_mix.shape[0]
    n = b * s
    p2 = p.reshape(depth, n, p.shape[-1])
    bf = lambda w: w.astype(_BF16)

    h = x.reshape(n, d)
    for i in range(depth):
        if i % 2 == 0:
            h = _pool_layer(h.reshape(b, s, d), g_mix[i], bf(w_pool[i // 2]),
                            pool_scale[i // 2]).reshape(n, d)
        else:
            xn = _rmsnorm(h, g_mix[i], _BF16)
            qkv = _matmul(xn, bf(w_qkv[i // 2]), (), _ep_identity, _BF16,
                          tm=1024, tn=1024, tk=4096, name="qkv_proj")
            o = _attention(qkv, b, s, d)
            h = _matmul(o, bf(w_o[i // 2]), (h,), _ep_residual, _F32,
                        tm=1024, tn=512, tk=4096, name="attn_out_proj")
        xn = _rmsnorm(h, g_mlp[i], _BF16)
        a = _matmul(xn, bf(w_up[i]), (), _ep_relu2, _BF16,
                    tm=1024, tn=1024, tk=4096, name="mlp_up")
        h = _matmul(a, bf(w_down[i]), (h,), _ep_residual, _F32,
                    tm=1024, tn=1024, tk=2048, name="mlp_down")
        e = _ple_embed(p2, i, bf(w_pe[i]), g_pe[i])
        xg = _rmsnorm(h, g_gate[i], _BF16)
        h = _matmul(xg, bf(w_gate[i]), (h, e), _ep_gate, _F32,
                    tm=1024, tn=512, tk=4096, name="ple_gate")
    return _rmsnorm(h, g_final, _F32).reshape(b, s, d)
```

```python
import functools
import math

import jax
import jax.numpy as jnp
from jax import lax
from jax.experimental import pallas as pl
from jax.experimental.pallas import tpu as pltpu

HEAD_DIM = 128
POOL_WINDOWS = (2, 4, 8, 16)
POOL_HALO = 16
RMS_EPS = 1e-6
_LOG2_E = 1.0 / math.log(2.0)

LANES = 128
V7X_VMEM_BYTES = 64 * 1024 * 1024
VMEM_LIMIT_BYTES = V7X_VMEM_BYTES - 8 * 1024 * 1024

_BF16 = jnp.bfloat16
_F32 = jnp.float32


def _tile(dim, pref):
    t = min(dim, pref)
    while dim % t:
        t //= 2
    return t


def _params(semantics):
    return pltpu.CompilerParams(dimension_semantics=semantics, vmem_limit_bytes=VMEM_LIMIT_BYTES)


def _rms(x, g):
    ms = jnp.mean(x * x, axis=-1, keepdims=True)
    return x * lax.rsqrt(ms + RMS_EPS) * g


def _lane_group_sum(x):
    total = x[:, 0:LANES]
    for c in range(1, x.shape[1] // LANES):
        total = total + x[:, c * LANES:(c + 1) * LANES]
    return total


def _row_sumsq(h):
    part = _lane_group_sum(h * h)
    return jnp.broadcast_to(jnp.sum(part, axis=-1, keepdims=True), part.shape)


def _row_scale(sumsq_ref, inv_d, width):
    r = lax.rsqrt(sumsq_ref[...] * inv_d + RMS_EPS)
    return jnp.concatenate([r] * (width // LANES), axis=1) if width > LANES else r


def _rmsnorm_kernel(x_ref, g_ref, o_ref):
    o_ref[...] = _rms(x_ref[...], g_ref[...]).astype(o_ref.dtype)


def _rmsnorm(x, g, out_dtype):
    n, d = x.shape
    tm = _tile(n, 256)
    return pl.pallas_call(
        _rmsnorm_kernel,
        grid=(n // tm,),
        in_specs=[pl.BlockSpec((tm, d), lambda i: (i, 0)),
                  pl.BlockSpec((1, d), lambda i: (0, 0))],
        out_specs=pl.BlockSpec((tm, d), lambda i: (i, 0)),
        out_shape=jax.ShapeDtypeStruct((n, d), out_dtype),
        compiler_params=_params(("parallel",)),
        name="rmsnorm",
    )(x, g.reshape(1, d))


def _ep_relu2(acc):
    return jnp.square(jnp.maximum(acc, 0.0))


def _ep_identity(acc):
    return acc


def _ep_residual(acc, res):
    return res + acc


def _ep_gate(acc, res, e):
    return res + jax.nn.sigmoid(acc) * e


def _matmul_kernel(*refs, nk, n_extra, epilogue, on_residual, inv_d_in, emit_norm):
    refs = list(refs)
    a_ref, w_ref = refs[:2]
    pos = 2
    sumsq_in_ref = None
    if inv_d_in is not None:
        sumsq_in_ref = refs[pos]
        pos += 1
    extras = refs[pos:pos + n_extra]
    pos += n_extra
    gain_ref = hg_ref = sumsq_ref = None
    if emit_norm:
        gain_ref = refs[pos]
        pos += 1
    o_ref = refs[pos]
    pos += 1
    if emit_norm:
        hg_ref, sumsq_ref = refs[pos:pos + 2]
        pos += 2
    j = pl.program_id(1)

    def partial_product():
        return jnp.dot(a_ref[...], w_ref[...], preferred_element_type=_F32)

    def emit(h_new):
        hg_ref[...] = (h_new * gain_ref[...]).astype(hg_ref.dtype)
        tile_sumsq = _row_sumsq(h_new)

        @pl.when(j == 0)
        def _():
            sumsq_ref[...] = tile_sumsq

        @pl.when(j > 0)
        def _():
            sumsq_ref[...] += tile_sumsq

    def finish(acc):
        if sumsq_in_ref is not None:
            acc = acc * _row_scale(sumsq_in_ref, inv_d_in, acc.shape[1])
        out = epilogue(acc, *[e[...] for e in extras])
        o_ref[...] = out.astype(o_ref.dtype)
        if emit_norm:
            emit(out)

    if nk == 1:
        finish(partial_product())
        return
    k = pl.program_id(2)
    if on_residual:
        @pl.when(k == 0)
        def _():
            o_ref[...] = extras[0][...]

        o_ref[...] += partial_product()
        if emit_norm:
            @pl.when(k == nk - 1)
            def _():
                emit(o_ref[...])
        return
    acc_ref = refs[pos]

    @pl.when(k == 0)
    def _():
        acc_ref[...] = jnp.zeros_like(acc_ref)

    acc_ref[...] += partial_product()

    @pl.when(k == nk - 1)
    def _():
        finish(acc_ref[...])


def _matmul(a, w, layer, *, epilogue, out_dtype, tm, tn, tk, name,
            sumsq_in=None, extras=(), next_gain=None):
    m, k = a.shape
    n = w.shape[-1]
    tm, tn, tk = _tile(m, tm), _tile(n, tn), _tile(k, tk)
    nk = k // tk
    emit_norm = next_gain is not None
    on_residual = nk > 1 and epilogue is _ep_residual and out_dtype == _F32 and sumsq_in is None
    tile_spec = pl.BlockSpec((tm, tn), lambda i, j, kk: (i, j))
    row_spec = pl.BlockSpec((tm, LANES), lambda i, j, kk: (i, 0))
    in_specs = [pl.BlockSpec((tm, tk), lambda i, j, kk: (i, kk)),
                pl.BlockSpec((None, tk, tn), lambda i, j, kk: (layer, kk, j))]
    operands = [a, w]
    if sumsq_in is not None:
        in_specs.append(row_spec)
        operands.append(sumsq_in)
    in_specs += [tile_spec] * len(extras)
    operands += list(extras)
    out_specs, out_shape = tile_spec, jax.ShapeDtypeStruct((m, n), out_dtype)
    if emit_norm:
        in_specs.append(pl.BlockSpec((1, tn), lambda i, j, kk: (0, j)))
        operands.append(next_gain.reshape(1, n))
        out_specs = [tile_spec, tile_spec, row_spec]
        out_shape = [out_shape, jax.ShapeDtypeStruct((m, n), _BF16),
                     jax.ShapeDtypeStruct((m, LANES), _F32)]
    kernel = functools.partial(
        _matmul_kernel, nk=nk, n_extra=len(extras), epilogue=epilogue, on_residual=on_residual,
        inv_d_in=None if sumsq_in is None else 1.0 / k, emit_norm=emit_norm)
    return pl.pallas_call(
        kernel,
        grid=(m // tm, n // tn, nk),
        in_specs=in_specs,
        out_specs=out_specs,
        out_shape=out_shape,
        scratch_shapes=[pltpu.VMEM((tm, tn), _F32)] if nk > 1 and not on_residual else [],
        compiler_params=_params(("parallel", "arbitrary", "arbitrary")),
        name=name,
    )(*operands)


def _pool_kernel(h_ref, halo_ref, g_ref, w_ref, scale_ref, gain_ref, o_ref, hg_ref, sumsq_ref,
                 xn_ref, *, ts, dg):
    i = pl.program_id(1)
    g = g_ref[...]
    h = h_ref[0]
    xn_ref[0:POOL_HALO, :] = jnp.where(i > 0, _rms(halo_ref[0], g), 0.0)
    xn_ref[POOL_HALO:, :] = _rms(h, g)
    pos = i * ts + lax.broadcasted_iota(jnp.int32, (ts, 1), 0)
    sumsq = None
    for gi, win in enumerate(POOL_WINDOWS):
        cols = slice(gi * dg, (gi + 1) * dg)
        cur = xn_ref[POOL_HALO:POOL_HALO + ts, cols]
        tot = cur
        for j in range(1, win):
            tot = tot + xn_ref[POOL_HALO - j:POOL_HALO - j + ts, cols]
        cnt = jnp.minimum(pos + 1, win).astype(_F32)
        pooled = tot / cnt - cur
        y = jnp.dot(pooled.astype(_BF16), w_ref[gi], preferred_element_type=_F32)
        h_new = h[:, cols] + y * scale_ref[:, cols]
        o_ref[0, :, cols] = h_new
        hg_ref[:, cols] = (h_new * gain_ref[:, cols]).astype(hg_ref.dtype)
        part = _lane_group_sum(h_new * h_new)
        sumsq = part if sumsq is None else sumsq + part
    sumsq_ref[...] = jnp.broadcast_to(jnp.sum(sumsq, axis=-1, keepdims=True), sumsq.shape)


def _pool_layer(h, g, w_pool, layer, scale, next_gain):
    b, s, d = h.shape
    ng = len(POOL_WINDOWS)
    dg = d // ng
    ts = _tile(s, 256)
    ns = s // ts
    halo_blocks = ts // POOL_HALO
    kernel = functools.partial(_pool_kernel, ts=ts, dg=dg)
    vec_spec = pl.BlockSpec((1, d), lambda bi, i: (0, 0))
    return pl.pallas_call(
        kernel,
        grid=(b, ns),
        in_specs=[
            pl.BlockSpec((1, ts, d), lambda bi, i: (bi, i, 0)),
            pl.BlockSpec((1, POOL_HALO, d),
                         lambda bi, i: (bi, jnp.maximum(i * halo_blocks - 1, 0), 0)),
            vec_spec,
            pl.BlockSpec((None, ng, dg, dg), lambda bi, i: (layer, 0, 0, 0)),
            vec_spec,
            vec_spec,
        ],
        out_specs=[pl.BlockSpec((1, ts, d), lambda bi, i: (bi, i, 0)),
                   pl.BlockSpec((ts, d), lambda bi, i: (bi * ns + i, 0)),
                   pl.BlockSpec((ts, LANES), lambda bi, i: (bi * ns + i, 0))],
        out_shape=[jax.ShapeDtypeStruct((b, s, d), _F32),
                   jax.ShapeDtypeStruct((b * s, d), _BF16),
                   jax.ShapeDtypeStruct((b * s, LANES), _F32)],
        scratch_shapes=[pltpu.VMEM((ts + POOL_HALO, d), _F32)],
        compiler_params=_params(("parallel", "parallel")),
        name="pool_mixer",
    )(h, h, g.reshape(1, d), w_pool, scale.reshape(1, d), next_gain.reshape(1, d))


ATTN_HEADS_PER_STEP = 8


def _attn_kernel(q_ref, k_ref, v_ref, o_ref, *, tq, nhb, log2_scale):
    qi = pl.program_id(2)
    row = lax.broadcasted_iota(jnp.int32, (tq, tq), 0)
    col = lax.broadcasted_iota(jnp.int32, (tq, tq), 1)
    from_key = jnp.where(row >= col, 1.0, 0.0).astype(_BF16)
    from_key2 = jnp.concatenate([from_key, from_key], axis=0)
    causal = col < row
    heads = [slice(hh * HEAD_DIM, (hh + 1) * HEAD_DIM) for hh in range(nhb)]
    sign_bit = jnp.uint32(0x80000000)

    def step(start, state, diagonal):
        zs = [lax.dot_general(q_ref[:, cols], k_ref[pl.ds(start, tq), cols],
                              (((1,), (1,)), ((), ())), preferred_element_type=_F32) * log2_scale
              for cols in heads]
        tail = []
        for z in zs:
            neg_abs = lax.bitcast_convert_type(lax.bitcast_convert_type(z, jnp.uint32) | sign_bit, _F32)
            sp = jnp.maximum(z, 0.0) + jnp.log(1.0 + jnp.exp2(neg_abs)) * _LOG2_E
            if diagonal:
                sp = jnp.where(causal, sp, 0.0)
            hi = sp.astype(_BF16)
            lo = (sp - hi.astype(_F32)).astype(_BF16)
            tail.append(jnp.dot(jnp.concatenate([hi, lo], axis=1), from_key2,
                                preferred_element_type=_F32))
        out = []
        for hh, cols in enumerate(heads):
            carry = state[2 * hh]
            attn = jnp.exp2(zs[hh] - carry - tail[hh])
            if diagonal:
                attn = jnp.where(causal, attn, 0.0)
            acc = state[2 * hh + 1] + jnp.dot(attn.astype(_BF16), v_ref[pl.ds(start, tq), cols],
                                              preferred_element_type=_F32)
            out.extend((carry + tail[hh][:, 0:1], acc))
        return tuple(out)

    init = (jnp.zeros((tq, 1), _F32), jnp.zeros((tq, HEAD_DIM), _F32)) * nhb
    state = step(pl.multiple_of(qi * tq, tq), init, True)

    def body(j, state):
        return step(pl.multiple_of((qi - 1 - j) * tq, tq), state, False)

    state = lax.fori_loop(0, qi, body, state)
    for hh, cols in enumerate(heads):
        o_ref[:, cols] = state[2 * hh + 1].astype(o_ref.dtype)


def _attention(qkv, b, s, d):
    nh = d // HEAD_DIM
    nhb = _tile(nh, ATTN_HEADS_PER_STEP)
    ng = nh // nhb
    tq = _tile(s, 256)
    nq = s // tq
    kernel = functools.partial(_attn_kernel, tq=tq, nhb=nhb,
                               log2_scale=_LOG2_E / math.sqrt(HEAD_DIM))
    wide = nhb * HEAD_DIM
    return pl.pallas_call(
        kernel,
        grid=(b, ng, nq),
        in_specs=[
            pl.BlockSpec((tq, wide), lambda bi, gi, qi: (bi * nq + qi, gi)),
            pl.BlockSpec((s, wide), lambda bi, gi, qi: (bi, ng + gi)),
            pl.BlockSpec((s, wide), lambda bi, gi, qi: (bi, 2 * ng + gi)),
        ],
        out_specs=pl.BlockSpec((tq, wide), lambda bi, gi, qi: (bi * nq + qi, gi)),
        out_shape=jax.ShapeDtypeStruct((b * s, d), _BF16),
        compiler_params=_params(("parallel", "parallel", "arbitrary")),
        name="stickbreak_attn",
    )(qkv, qkv, qkv)


def _ple_kernel(p_ref, w_ref, g_ref, o_ref):
    y = jnp.dot(p_ref[...].astype(_BF16), w_ref[...], preferred_element_type=_F32)
    o_ref[...] = _rms(y, g_ref[...])


def _ple_embed(p, layer, w_pe, g_pe):
    _, n, pd = p.shape
    d = w_pe.shape[-1]
    tm = _tile(n, 256)
    return pl.pallas_call(
        _ple_kernel,
        grid=(n // tm,),
        in_specs=[pl.BlockSpec((None, tm, pd), lambda i: (layer, i, 0)),
                  pl.BlockSpec((None, pd, d), lambda i: (layer, 0, 0)),
                  pl.BlockSpec((1, d), lambda i: (0, 0))],
        out_specs=pl.BlockSpec((tm, d), lambda i: (i, 0)),
        out_shape=jax.ShapeDtypeStruct((n, d), _F32),
        compiler_params=_params(("parallel",)),
        name="ple_embed",
    )(p, w_pe, g_pe.reshape(1, d))


def kernel(x, p, g_mix, g_mlp, w_pool, pool_scale, w_qkv, w_o, w_up, w_down,
           w_pe, g_pe, g_gate, w_gate, g_final):
    b, s, d = x.shape
    depth = g_mix.shape[0]
    n = b * s
    p2 = p.reshape(depth, n, p.shape[-1])
    w_pool, w_qkv, w_o, w_up, w_down, w_pe, w_gate = (
        w.astype(_BF16) for w in (w_pool, w_qkv, w_o, w_up, w_down, w_pe, w_gate))

    h = x.reshape(n, d)
    hg = sumsq = None
    for i in range(depth):
        if i % 2 == 0:
            h, hg, sumsq = _pool_layer(h.reshape(b, s, d), g_mix[i], w_pool, i // 2,
                                       pool_scale[i // 2], g_mlp[i])
            h = h.reshape(n, d)
        else:
            qkv = _matmul(hg, w_qkv, i // 2, sumsq_in=sumsq, epilogue=_ep_identity, out_dtype=_BF16,
                          tm=1024, tn=1024, tk=4096, name="qkv_proj")
            o = _attention(qkv, b, s, d)
            h, hg, sumsq = _matmul(o, w_o, i // 2, extras=(h,), epilogue=_ep_residual,
                                   out_dtype=_F32, next_gain=g_mlp[i],
                                   tm=1024, tn=512, tk=4096, name="attn_out_proj")
        a = _matmul(hg, w_up, i, sumsq_in=sumsq, epilogue=_ep_relu2, out_dtype=_BF16,
                    tm=1024, tn=1024, tk=4096, name="mlp_up")
        h, hg, sumsq = _matmul(a, w_down, i, extras=(h,), epilogue=_ep_residual, out_dtype=_F32,
                               next_gain=g_gate[i], tm=1024, tn=1024, tk=2048, name="mlp_down")
        e = _ple_embed(p2, i, w_pe, g_pe[i])
        feeds_attention = i + 1 < depth and (i + 1) % 2 == 1
        out = _matmul(hg, w_gate, i, sumsq_in=sumsq, extras=(h, e), epilogue=_ep_gate,
                      out_dtype=_F32, next_gain=g_mix[i + 1] if feeds_attention else None,
                      tm=1024, tn=512, tk=4096, name="ple_gate")
        h, hg, sumsq = out if feeds_attention else (out, None, None)
    return _rmsnorm(h, g_final, _F32).reshape(b, s, d)
```

```python
import functools
import math

import jax
import jax.numpy as jnp
from jax import lax
from jax.experimental import pallas as pl
from jax.experimental.pallas import tpu as pltpu

HEAD_DIM = 128
POOL_WINDOWS = (2, 4, 8, 16)
POOL_HALO = 16
RMS_EPS = 1e-6
_LOG2_E = 1.0 / math.log(2.0)

LANES = 128
BF16_SUBLANES = 16
V7X_VMEM_BYTES = 64 * 1024 * 1024
VMEM_LIMIT_BYTES = V7X_VMEM_BYTES - 8 * 1024 * 1024

_BF16 = jnp.bfloat16
_F32 = jnp.float32


def _tile(dim, pref):
    t = min(dim, pref)
    while dim % t:
        t //= 2
    return t


def _params(semantics):
    return pltpu.CompilerParams(dimension_semantics=semantics, vmem_limit_bytes=VMEM_LIMIT_BYTES)


def _rms(x, g):
    ms = jnp.mean(x * x, axis=-1, keepdims=True)
    return x * lax.rsqrt(ms + RMS_EPS) * g


def _lane_group_sum(x):
    total = x[:, 0:LANES]
    for c in range(1, x.shape[1] // LANES):
        total = total + x[:, c * LANES:(c + 1) * LANES]
    return total


def _row_sumsq(h):
    part = _lane_group_sum(h * h)
    return jnp.broadcast_to(jnp.sum(part, axis=-1, keepdims=True), part.shape)


def _row_scale(sumsq_ref, inv_d, width):
    r = lax.rsqrt(sumsq_ref[...] * inv_d + RMS_EPS)
    return jnp.concatenate([r] * (width // LANES), axis=1) if width > LANES else r


def _rmsnorm_kernel(x_ref, g_ref, o_ref):
    o_ref[...] = _rms(x_ref[...], g_ref[...]).astype(o_ref.dtype)


def _rmsnorm(x, g, out_dtype):
    n, d = x.shape
    tm = _tile(n, 256)
    return pl.pallas_call(
        _rmsnorm_kernel,
        grid=(n // tm,),
        in_specs=[pl.BlockSpec((tm, d), lambda i: (i, 0)),
                  pl.BlockSpec((1, d), lambda i: (0, 0))],
        out_specs=pl.BlockSpec((tm, d), lambda i: (i, 0)),
        out_shape=jax.ShapeDtypeStruct((n, d), out_dtype),
        compiler_params=_params(("parallel",)),
        name="rmsnorm",
    )(x, g.reshape(1, d))


def _ep_relu2(acc):
    return jnp.square(jnp.maximum(acc, 0.0))


def _ep_identity(acc):
    return acc


def _ep_residual(acc, res):
    return res + acc


def _ep_gate(acc, res, e):
    return res + jax.nn.sigmoid(acc) * e


def _matmul_kernel(*refs, nk, n_extra, epilogue, on_residual, inv_d_in, col_scaled, emit_norm,
                   n_casts):
    refs = list(refs)
    a_ref, w_ref = refs[:2]
    pos = 2
    sumsq_in_ref = col_scale_ref = None
    if inv_d_in is not None:
        sumsq_in_ref = refs[pos]
        pos += 1
    if col_scaled:
        col_scale_ref = refs[pos]
        pos += 1
    extras = refs[pos:pos + n_extra]
    pos += n_extra
    gain_ref = hg_ref = sumsq_ref = None
    if emit_norm:
        gain_ref = refs[pos]
        pos += 1
    cast_srcs = refs[pos:pos + n_casts]
    pos += n_casts
    o_ref = refs[pos]
    pos += 1
    if emit_norm:
        hg_ref, sumsq_ref = refs[pos:pos + 2]
        pos += 2
    cast_dsts = refs[pos:pos + n_casts]
    pos += n_casts
    j = pl.program_id(1)

    def partial_product():
        for src, dst in zip(cast_srcs, cast_dsts):
            dst[...] = src[...].astype(dst.dtype)
        return jnp.dot(a_ref[...], w_ref[...], preferred_element_type=_F32)

    def emit(h_new):
        hg_ref[...] = (h_new * gain_ref[...]).astype(hg_ref.dtype)
        tile_sumsq = _row_sumsq(h_new)

        @pl.when(j == 0)
        def _():
            sumsq_ref[...] = tile_sumsq

        @pl.when(j > 0)
        def _():
            sumsq_ref[...] += tile_sumsq

    def finish(acc):
        if sumsq_in_ref is not None:
            acc = acc * _row_scale(sumsq_in_ref, inv_d_in, acc.shape[1])
        if col_scale_ref is not None:
            acc = acc * col_scale_ref[...]
        out = epilogue(acc, *[e[...] for e in extras])
        o_ref[...] = out.astype(o_ref.dtype)
        if emit_norm:
            emit(out)

    if nk == 1:
        finish(partial_product())
        return
    k = pl.program_id(2)
    if on_residual:
        @pl.when(k == 0)
        def _():
            o_ref[...] = extras[0][...] + partial_product()

        @pl.when(jnp.logical_and(k > 0, k < nk - 1))
        def _():
            o_ref[...] += partial_product()

        @pl.when(k == nk - 1)
        def _():
            out = o_ref[...] + partial_product()
            o_ref[...] = out
            if emit_norm:
                emit(out)
        return
    acc_ref = refs[pos]

    @pl.when(k == 0)
    def _():
        acc_ref[...] = jnp.zeros_like(acc_ref)

    acc_ref[...] += partial_product()

    @pl.when(k == nk - 1)
    def _():
        finish(acc_ref[...])


def _cast_blocks(rows, n_steps):
    n_blocks = 1
    while (n_blocks * 2 <= n_steps and rows % (n_blocks * 2) == 0
           and rows // (n_blocks * 2) >= BF16_SUBLANES):
        n_blocks *= 2
    return rows // n_blocks, n_blocks


def _matmul(a, w, *, epilogue, out_dtype, tm, tn, tk, name,
            sumsq_in=None, col_scale=None, extras=(), next_gain=None, casts=()):
    m, k = a.shape
    n = w.shape[-1]
    tm, tn, tk = _tile(m, tm), _tile(n, tn), _tile(k, tk)
    nj, nk = n // tn, k // tk
    n_steps = (m // tm) * nj * nk
    emit_norm = next_gain is not None
    on_residual = (nk > 1 and epilogue is _ep_residual and out_dtype == _F32
                   and sumsq_in is None and col_scale is None)
    tile_spec = pl.BlockSpec((tm, tn), lambda i, j, kk: (i, j))
    row_spec = pl.BlockSpec((tm, LANES), lambda i, j, kk: (i, 0))
    col_spec = pl.BlockSpec((1, tn), lambda i, j, kk: (0, j))
    in_specs = [pl.BlockSpec((tm, tk), lambda i, j, kk: (i, kk)),
                pl.BlockSpec((tk, tn), lambda i, j, kk: (kk, j))]
    operands = [a, w]
    if sumsq_in is not None:
        in_specs.append(row_spec)
        operands.append(sumsq_in)
    if col_scale is not None:
        in_specs.append(col_spec)
        operands.append(col_scale.reshape(1, n))
    in_specs += [tile_spec] * len(extras)
    operands += list(extras)
    out_specs = [tile_spec]
    out_shape = [jax.ShapeDtypeStruct((m, n), out_dtype)]
    if emit_norm:
        in_specs.append(col_spec)
        operands.append(next_gain.reshape(1, n))
        out_specs += [tile_spec, row_spec]
        out_shape += [jax.ShapeDtypeStruct((m, n), _BF16), jax.ShapeDtypeStruct((m, LANES), _F32)]
    for src, src_layer in casts:
        _, rows, cols = src.shape
        rb, n_blocks = _cast_blocks(rows, n_steps)

        def block_of(i, j, kk, n_blocks=n_blocks):
            return (((i * nj + j) * nk + kk) * n_blocks) // n_steps

        in_specs.append(pl.BlockSpec(
            (None, rb, cols), lambda i, j, kk, f=block_of, l=src_layer: (l, f(i, j, kk), 0)))
        operands.append(src)
        out_specs.append(pl.BlockSpec((rb, cols), lambda i, j, kk, f=block_of: (f(i, j, kk), 0)))
        out_shape.append(jax.ShapeDtypeStruct((rows, cols), _BF16))
    kernel = functools.partial(
        _matmul_kernel, nk=nk, n_extra=len(extras), epilogue=epilogue, on_residual=on_residual,
        inv_d_in=None if sumsq_in is None else 1.0 / k, col_scaled=col_scale is not None,
        emit_norm=emit_norm, n_casts=len(casts))
    outs = pl.pallas_call(
        kernel,
        grid=(m // tm, nj, nk),
        in_specs=in_specs,
        out_specs=out_specs,
        out_shape=out_shape,
        scratch_shapes=[pltpu.VMEM((tm, tn), _F32)] if nk > 1 and not on_residual else [],
        compiler_params=_params(("arbitrary", "arbitrary", "arbitrary")),
        name=name,
    )(*operands)
    return outs[0] if len(outs) == 1 else tuple(outs)


def _pool_kernel(h_ref, halo_ref, g_ref, w_ref, scale_ref, gain_ref, o_ref, hg_ref, sumsq_ref,
                 xn_ref, *, ts, dg):
    i = pl.program_id(1)
    g = g_ref[...]
    h = h_ref[0]
    xn_ref[0:POOL_HALO, :] = jnp.where(i > 0, _rms(halo_ref[0], g), 0.0)
    xn_ref[POOL_HALO:, :] = _rms(h, g)
    pos = i * ts + lax.broadcasted_iota(jnp.int32, (ts, 1), 0)
    sumsq = None
    for gi, win in enumerate(POOL_WINDOWS):
        cols = slice(gi * dg, (gi + 1) * dg)
        cur = xn_ref[POOL_HALO:POOL_HALO + ts, cols]
        tot = cur
        for j in range(1, win):
            tot = tot + xn_ref[POOL_HALO - j:POOL_HALO - j + ts, cols]
        cnt = jnp.minimum(pos + 1, win).astype(_F32)
        pooled = tot / cnt - cur
        y = jnp.dot(pooled.astype(_BF16), w_ref[gi], preferred_element_type=_F32)
        h_new = h[:, cols] + y * scale_ref[:, cols]
        o_ref[0, :, cols] = h_new
        hg_ref[:, cols] = (h_new * gain_ref[:, cols]).astype(hg_ref.dtype)
        part = _lane_group_sum(h_new * h_new)
        sumsq = part if sumsq is None else sumsq + part
    sumsq_ref[...] = jnp.broadcast_to(jnp.sum(sumsq, axis=-1, keepdims=True), sumsq.shape)


def _pool_layer(h, g, w_pool, layer, scale, next_gain):
    b, s, d = h.shape
    ng = len(POOL_WINDOWS)
    dg = d // ng
    ts = _tile(s, 256)
    ns = s // ts
    halo_blocks = ts // POOL_HALO
    kernel = functools.partial(_pool_kernel, ts=ts, dg=dg)
    vec_spec = pl.BlockSpec((1, d), lambda bi, i: (0, 0))
    return pl.pallas_call(
        kernel,
        grid=(b, ns),
        in_specs=[
            pl.BlockSpec((1, ts, d), lambda bi, i: (bi, i, 0)),
            pl.BlockSpec((1, POOL_HALO, d),
                         lambda bi, i: (bi, jnp.maximum(i * halo_blocks - 1, 0), 0)),
            vec_spec,
            pl.BlockSpec((None, ng, dg, dg), lambda bi, i: (layer, 0, 0, 0)),
            vec_spec,
            vec_spec,
        ],
        out_specs=[pl.BlockSpec((1, ts, d), lambda bi, i: (bi, i, 0)),
                   pl.BlockSpec((ts, d), lambda bi, i: (bi * ns + i, 0)),
                   pl.BlockSpec((ts, LANES), lambda bi, i: (bi * ns + i, 0))],
        out_shape=[jax.ShapeDtypeStruct((b, s, d), _F32),
                   jax.ShapeDtypeStruct((b * s, d), _BF16),
                   jax.ShapeDtypeStruct((b * s, LANES), _F32)],
        scratch_shapes=[pltpu.VMEM((ts + POOL_HALO, d), _F32)],
        compiler_params=_params(("parallel", "parallel")),
        name="pool_mixer",
    )(h, h, g.reshape(1, d), w_pool, scale.reshape(1, d), next_gain.reshape(1, d))


ATTN_HEADS_PER_STEP = 8
ATTN_Q_SCALE = _LOG2_E / math.sqrt(HEAD_DIM)


def _attn_kernel(q_ref, k_ref, v_ref, o_ref, *, tq, nhb):
    qi = pl.program_id(2)
    row = lax.broadcasted_iota(jnp.int32, (tq, tq), 0)
    col = lax.broadcasted_iota(jnp.int32, (tq, tq), 1)
    from_key = jnp.where(row >= col, 1.0, 0.0).astype(_BF16)
    from_key2 = jnp.concatenate([from_key, from_key], axis=0)
    causal = col < row
    heads = [slice(hh * HEAD_DIM, (hh + 1) * HEAD_DIM) for hh in range(nhb)]
    sign_bit = jnp.uint32(0x80000000)

    def step(start, state, diagonal):
        zs = [lax.dot_general(q_ref[:, cols], k_ref[pl.ds(start, tq), cols],
                              (((1,), (1,)), ((), ())), preferred_element_type=_F32)
              for cols in heads]
        tail = []
        for z in zs:
            neg_abs = lax.bitcast_convert_type(lax.bitcast_convert_type(z, jnp.uint32) | sign_bit, _F32)
            sp = jnp.maximum(z, 0.0) + jnp.log(1.0 + jnp.exp2(neg_abs)) * _LOG2_E
            if diagonal:
                sp = jnp.where(causal, sp, 0.0)
            hi = sp.astype(_BF16)
            lo = (sp - hi.astype(_F32)).astype(_BF16)
            tail.append(jnp.dot(jnp.concatenate([hi, lo], axis=1), from_key2,
                                preferred_element_type=_F32))
        out = []
        for hh, cols in enumerate(heads):
            carry = state[2 * hh]
            attn = jnp.exp2(zs[hh] - carry - tail[hh])
            if diagonal:
                attn = jnp.where(causal, attn, 0.0)
            acc = state[2 * hh + 1] + jnp.dot(attn.astype(_BF16), v_ref[pl.ds(start, tq), cols],
                                              preferred_element_type=_F32)
            out.extend((carry + tail[hh][:, 0:1], acc))
        return tuple(out)

    init = (jnp.zeros((tq, 1), _F32), jnp.zeros((tq, HEAD_DIM), _F32)) * nhb
    state = step(pl.multiple_of(qi * tq, tq), init, True)

    def body(j, state):
        return step(pl.multiple_of((qi - 1 - j) * tq, tq), state, False)

    state = lax.fori_loop(0, qi, body, state)
    for hh, cols in enumerate(heads):
        o_ref[:, cols] = state[2 * hh + 1].astype(o_ref.dtype)


def _attention(qkv, b, s, d):
    nh = d // HEAD_DIM
    nhb = _tile(nh, ATTN_HEADS_PER_STEP)
    ng = nh // nhb
    tq = _tile(s, 256)
    nq = s // tq
    kernel = functools.partial(_attn_kernel, tq=tq, nhb=nhb)
    wide = nhb * HEAD_DIM
    return pl.pallas_call(
        kernel,
        grid=(b, ng, nq),
        in_specs=[
            pl.BlockSpec((tq, wide), lambda bi, gi, qi: (bi * nq + qi, gi)),
            pl.BlockSpec((s, wide), lambda bi, gi, qi: (bi, ng + gi)),
            pl.BlockSpec((s, wide), lambda bi, gi, qi: (bi, 2 * ng + gi)),
        ],
        out_specs=pl.BlockSpec((tq, wide), lambda bi, gi, qi: (bi * nq + qi, gi)),
        out_shape=jax.ShapeDtypeStruct((b * s, d), _BF16),
        compiler_params=_params(("parallel", "parallel", "arbitrary")),
        name="stickbreak_attn",
    )(qkv, qkv, qkv)


def _ple_kernel(p_ref, w_ref, g_ref, o_ref):
    y = jnp.dot(p_ref[...].astype(_BF16), w_ref[...], preferred_element_type=_F32)
    o_ref[...] = _rms(y, g_ref[...])


def _ple_embed(p, layer, w_pe, g_pe):
    _, n, pd = p.shape
    d = w_pe.shape[-1]
    tm = _tile(n, 256)
    return pl.pallas_call(
        _ple_kernel,
        grid=(n // tm,),
        in_specs=[pl.BlockSpec((None, tm, pd), lambda i: (layer, i, 0)),
                  pl.BlockSpec((None, pd, d), lambda i: (layer, 0, 0)),
                  pl.BlockSpec((1, d), lambda i: (0, 0))],
        out_specs=pl.BlockSpec((tm, d), lambda i: (i, 0)),
        out_shape=jax.ShapeDtypeStruct((n, d), _F32),
        compiler_params=_params(("parallel",)),
        name="ple_embed",
    )(p, w_pe, g_pe.reshape(1, d))


def kernel(x, p, g_mix, g_mlp, w_pool, pool_scale, w_qkv, w_o, w_up, w_down,
           w_pe, g_pe, g_gate, w_gate, g_final):
    b, s, d = x.shape
    depth = g_mix.shape[0]
    n = b * s
    p2 = p.reshape(depth, n, p.shape[-1])
    w_pool_b, w_pe_b = w_pool.astype(_BF16), w_pe.astype(_BF16)
    w_up_b = w_up[0].astype(_BF16)
    w_qkv_b = w_o_b = None
    q_scale = jnp.concatenate([jnp.full((d,), ATTN_Q_SCALE, _F32), jnp.ones((2 * d,), _F32)])

    h = x.reshape(n, d)
    hg = sumsq = None
    for i in range(depth):
        if i % 2 == 0:
            h, hg, sumsq = _pool_layer(h.reshape(b, s, d), g_mix[i], w_pool_b, i // 2,
                                       pool_scale[i // 2], g_mlp[i])
            h = h.reshape(n, d)
        else:
            qkv = _matmul(hg, w_qkv_b, sumsq_in=sumsq, col_scale=q_scale, epilogue=_ep_identity,
                          out_dtype=_BF16, tm=1024, tn=1024, tk=4096, name="qkv_proj")
            o = _attention(qkv, b, s, d)
            h, hg, sumsq = _matmul(o, w_o_b, extras=(h,), epilogue=_ep_residual,
                                   out_dtype=_F32, next_gain=g_mlp[i],
                                   tm=1024, tn=512, tk=4096, name="attn_out_proj")
        feeds_attention = i + 1 < depth and (i + 1) % 2 == 1
        casts = [(w_down, i)]
        if feeds_attention:
            casts += [(w_qkv, (i + 1) // 2), (w_o, (i + 1) // 2)]
        outs = _matmul(hg, w_up_b, sumsq_in=sumsq, epilogue=_ep_relu2, out_dtype=_BF16,
                       casts=casts, tm=1024, tn=1024, tk=4096, name="mlp_up")
        a, w_down_b = outs[:2]
        if feeds_attention:
            w_qkv_b, w_o_b = outs[2:]
        casts = [(w_gate, i)] + ([(w_up, i + 1)] if i + 1 < depth else [])
        outs = _matmul(a, w_down_b, extras=(h,), epilogue=_ep_residual, out_dtype=_F32,
                       next_gain=g_gate[i], casts=casts,
                       tm=1024, tn=1024, tk=2048, name="mlp_down")
        h, hg, sumsq, w_gate_b = outs[:4]
        if i + 1 < depth:
            w_up_b = outs[4]
        e = _ple_embed(p2, i, w_pe_b, g_pe[i])
        out = _matmul(hg, w_gate_b, sumsq_in=sumsq, extras=(h, e), epilogue=_ep_gate,
                      out_dtype=_F32, next_gain=g_mix[i + 1] if feeds_attention else None,
                      tm=1024, tn=512, tk=4096, name="ple_gate")
        h, hg, sumsq = out if feeds_attention else (out, None, None)
    return _rmsnorm(h, g_final, _F32).reshape(b, s, d)
```

```python
import functools
import math

import jax
import jax.numpy as jnp
from jax import lax
from jax.experimental import pallas as pl
from jax.experimental.pallas import tpu as pltpu

HEAD_DIM = 128
POOL_WINDOWS = (2, 4, 8, 16)
POOL_HALO = 16
RMS_EPS = 1e-6
_LOG2_E = 1.0 / math.log(2.0)

LANES = 128
SUBLANES = 8
BF16_SUBLANES = 16
V7X_VMEM_BYTES = 64 * 1024 * 1024
VMEM_LIMIT_BYTES = V7X_VMEM_BYTES - 8 * 1024 * 1024

_BF16 = jnp.bfloat16
_F32 = jnp.float32


def _tile(dim, pref):
    t = min(dim, pref)
    while dim % t:
        t //= 2
    return t


def _params(semantics):
    return pltpu.CompilerParams(dimension_semantics=semantics, vmem_limit_bytes=VMEM_LIMIT_BYTES)


def _rms(x, g):
    ms = jnp.mean(x * x, axis=-1, keepdims=True)
    return x * lax.rsqrt(ms + RMS_EPS) * g


def _lane_group_sum(x):
    total = x[:, 0:LANES]
    for c in range(1, x.shape[1] // LANES):
        total = total + x[:, c * LANES:(c + 1) * LANES]
    return total


def _row_sumsq(h):
    part = _lane_group_sum(h * h)
    return jnp.broadcast_to(jnp.sum(part, axis=-1, keepdims=True), part.shape)


def _row_scale(sumsq_ref, inv_d, width):
    r = lax.rsqrt(sumsq_ref[...] * inv_d + RMS_EPS)
    return jnp.concatenate([r] * (width // LANES), axis=1) if width > LANES else r


def _rmsnorm_kernel(x_ref, g_ref, o_ref):
    o_ref[...] = _rms(x_ref[...], g_ref[...]).astype(o_ref.dtype)


def _rmsnorm(x, g, out_dtype):
    n, d = x.shape
    tm = _tile(n, 256)
    return pl.pallas_call(
        _rmsnorm_kernel,
        grid=(n // tm,),
        in_specs=[pl.BlockSpec((tm, d), lambda i: (i, 0)),
                  pl.BlockSpec((1, d), lambda i: (0, 0))],
        out_specs=pl.BlockSpec((tm, d), lambda i: (i, 0)),
        out_shape=jax.ShapeDtypeStruct((n, d), out_dtype),
        compiler_params=_params(("parallel",)),
        name="rmsnorm",
    )(x, g.reshape(1, d))


def _ep_relu2(acc):
    return jnp.square(jnp.maximum(acc, 0.0))


def _ep_identity(acc):
    return acc


def _ep_residual(acc, res):
    return res + acc


def _ep_gate(acc, res, e):
    return res + jax.nn.sigmoid(acc) * e


def _matmul_kernel(*refs, nk, n_extra, epilogue, on_residual, inv_d_in, col_scaled, emit_norm,
                   n_casts):
    refs = list(refs)
    a_ref, w_ref = refs[:2]
    pos = 2
    sumsq_in_ref = col_scale_ref = None
    if inv_d_in is not None:
        sumsq_in_ref = refs[pos]
        pos += 1
    if col_scaled:
        col_scale_ref = refs[pos]
        pos += 1
    extras = refs[pos:pos + n_extra]
    pos += n_extra
    gain_ref = hg_ref = sumsq_ref = None
    if emit_norm:
        gain_ref = refs[pos]
        pos += 1
    cast_srcs = refs[pos:pos + n_casts]
    pos += n_casts
    o_ref = refs[pos]
    pos += 1
    if emit_norm:
        hg_ref, sumsq_ref = refs[pos:pos + 2]
        pos += 2
    cast_dsts = refs[pos:pos + n_casts]
    pos += n_casts
    j = pl.program_id(1)

    def partial_product():
        for src, dst in zip(cast_srcs, cast_dsts):
            dst[...] = src[...].astype(dst.dtype)
        return jnp.dot(a_ref[...], w_ref[...], preferred_element_type=_F32)

    def emit(h_new):
        hg_ref[...] = (h_new * gain_ref[...]).astype(hg_ref.dtype)
        tile_sumsq = _row_sumsq(h_new)

        @pl.when(j == 0)
        def _():
            sumsq_ref[...] = tile_sumsq

        @pl.when(j > 0)
        def _():
            sumsq_ref[...] += tile_sumsq

    def finish(acc):
        if sumsq_in_ref is not None:
            acc = acc * _row_scale(sumsq_in_ref, inv_d_in, acc.shape[1])
        if col_scale_ref is not None:
            acc = acc * col_scale_ref[...]
        out = epilogue(acc, *[e[...] for e in extras])
        o_ref[...] = out.astype(o_ref.dtype)
        if emit_norm:
            emit(out)

    if nk == 1:
        finish(partial_product())
        return
    k = pl.program_id(2)
    if on_residual:
        @pl.when(k == 0)
        def _():
            o_ref[...] = extras[0][...] + partial_product()

        @pl.when(jnp.logical_and(k > 0, k < nk - 1))
        def _():
            o_ref[...] += partial_product()

        @pl.when(k == nk - 1)
        def _():
            out = o_ref[...] + partial_product()
            o_ref[...] = out
            if emit_norm:
                emit(out)
        return
    acc_ref = refs[pos]

    @pl.when(k == 0)
    def _():
        acc_ref[...] = jnp.zeros_like(acc_ref)

    acc_ref[...] += partial_product()

    @pl.when(k == nk - 1)
    def _():
        finish(acc_ref[...])


def _cast_blocks(rows, n_steps):
    n_blocks = 1
    while (n_blocks * 2 <= n_steps and rows % (n_blocks * 2) == 0
           and rows // (n_blocks * 2) >= BF16_SUBLANES):
        n_blocks *= 2
    return rows // n_blocks, n_blocks


def _matmul(a, w, *, epilogue, out_dtype, tm, tn, tk, name,
            sumsq_in=None, col_scale=None, extras=(), next_gain=None, casts=()):
    m, k = a.shape
    n = w.shape[-1]
    tm, tn, tk = _tile(m, tm), _tile(n, tn), _tile(k, tk)
    nj, nk = n // tn, k // tk
    n_steps = (m // tm) * nj * nk
    emit_norm = next_gain is not None
    on_residual = (nk > 1 and epilogue is _ep_residual and out_dtype == _F32
                   and sumsq_in is None and col_scale is None)
    tile_spec = pl.BlockSpec((tm, tn), lambda i, j, kk: (i, j))
    row_spec = pl.BlockSpec((tm, LANES), lambda i, j, kk: (i, 0))
    col_spec = pl.BlockSpec((1, tn), lambda i, j, kk: (0, j))
    in_specs = [pl.BlockSpec((tm, tk), lambda i, j, kk: (i, kk)),
                pl.BlockSpec((tk, tn), lambda i, j, kk: (kk, j))]
    operands = [a, w]
    if sumsq_in is not None:
        in_specs.append(row_spec)
        operands.append(sumsq_in)
    if col_scale is not None:
        in_specs.append(col_spec)
        operands.append(col_scale.reshape(1, n))
    in_specs += [tile_spec] * len(extras)
    operands += list(extras)
    out_specs = [tile_spec]
    out_shape = [jax.ShapeDtypeStruct((m, n), out_dtype)]
    if emit_norm:
        in_specs.append(col_spec)
        operands.append(next_gain.reshape(1, n))
        out_specs += [tile_spec, row_spec]
        out_shape += [jax.ShapeDtypeStruct((m, n), _BF16), jax.ShapeDtypeStruct((m, LANES), _F32)]
    for src, src_layer in casts:
        _, rows, cols = src.shape
        rb, n_blocks = _cast_blocks(rows, n_steps)

        def block_of(i, j, kk, n_blocks=n_blocks):
            return (((i * nj + j) * nk + kk) * n_blocks) // n_steps

        in_specs.append(pl.BlockSpec(
            (None, rb, cols), lambda i, j, kk, f=block_of, l=src_layer: (l, f(i, j, kk), 0)))
        operands.append(src)
        out_specs.append(pl.BlockSpec((rb, cols), lambda i, j, kk, f=block_of: (f(i, j, kk), 0)))
        out_shape.append(jax.ShapeDtypeStruct((rows, cols), _BF16))
    kernel = functools.partial(
        _matmul_kernel, nk=nk, n_extra=len(extras), epilogue=epilogue, on_residual=on_residual,
        inv_d_in=None if sumsq_in is None else 1.0 / k, col_scaled=col_scale is not None,
        emit_norm=emit_norm, n_casts=len(casts))
    outs = pl.pallas_call(
        kernel,
        grid=(m // tm, nj, nk),
        in_specs=in_specs,
        out_specs=out_specs,
        out_shape=out_shape,
        scratch_shapes=[pltpu.VMEM((tm, tn), _F32)] if nk > 1 and not on_residual else [],
        compiler_params=_params(("arbitrary", "arbitrary", "arbitrary")),
        name=name,
    )(*operands)
    return outs[0] if len(outs) == 1 else tuple(outs)


def _pool_kernel(h_ref, halo_ref, g_ref, w_ref, scale_ref, gain_ref, o_ref, hg_ref, sumsq_ref,
                 xn_ref, lvl_a_ref, lvl_b_ref, *, ts, dg):
    i = pl.program_id(1)
    g = g_ref[...]
    h = h_ref[0]
    first = SUBLANES
    tile0 = first + POOL_HALO
    end = tile0 + ts
    xn_ref[0:first, :] = jnp.zeros((first, xn_ref.shape[1]), _F32)
    xn_ref[first:tile0, :] = jnp.where(i > 0, _rms(halo_ref[0], g), 0.0)
    xn_ref[tile0:, :] = _rms(h, g)
    for lvl_ref in (lvl_a_ref, lvl_b_ref):
        lvl_ref[0:first, :] = jnp.zeros((first, dg), _F32)
    pos = i * ts + lax.broadcasted_iota(jnp.int32, (ts, 1), 0)
    sumsq = None
    for gi, win in enumerate(POOL_WINDOWS):
        cols = slice(gi * dg, (gi + 1) * dg)
        cur = xn_ref[tile0:end, cols]
        src = lambda lo, hi, cols=cols: xn_ref[lo:hi, cols]
        width, levels = 1, [lvl_a_ref, lvl_b_ref]
        while 2 * width < win:
            dst = levels[0]
            dst[first:end, :] = src(first, end) + src(first - width, end - width)
            src = lambda lo, hi, dst=dst: dst[lo:hi, :]
            levels.reverse()
            width *= 2
        tot = src(tile0, end) + src(tile0 - width, end - width)
        cnt = jnp.minimum(pos + 1, win).astype(_F32)
        pooled = tot / cnt - cur
        y = jnp.dot(pooled.astype(_BF16), w_ref[gi], preferred_element_type=_F32)
        h_new = h[:, cols] + y * scale_ref[:, cols]
        o_ref[0, :, cols] = h_new
        hg_ref[:, cols] = (h_new * gain_ref[:, cols]).astype(hg_ref.dtype)
        part = _lane_group_sum(h_new * h_new)
        sumsq = part if sumsq is None else sumsq + part
    sumsq_ref[...] = jnp.broadcast_to(jnp.sum(sumsq, axis=-1, keepdims=True), sumsq.shape)


def _pool_layer(h, g, w_pool, layer, scale, next_gain):
    b, s, d = h.shape
    ng = len(POOL_WINDOWS)
    dg = d // ng
    ts = _tile(s, 256)
    ns = s // ts
    halo_blocks = ts // POOL_HALO
    rows = SUBLANES + POOL_HALO + ts
    kernel = functools.partial(_pool_kernel, ts=ts, dg=dg)
    vec_spec = pl.BlockSpec((1, d), lambda bi, i: (0, 0))
    return pl.pallas_call(
        kernel,
        grid=(b, ns),
        in_specs=[
            pl.BlockSpec((1, ts, d), lambda bi, i: (bi, i, 0)),
            pl.BlockSpec((1, POOL_HALO, d),
                         lambda bi, i: (bi, jnp.maximum(i * halo_blocks - 1, 0), 0)),
            vec_spec,
            pl.BlockSpec((None, ng, dg, dg), lambda bi, i: (layer, 0, 0, 0)),
            vec_spec,
            vec_spec,
        ],
        out_specs=[pl.BlockSpec((1, ts, d), lambda bi, i: (bi, i, 0)),
                   pl.BlockSpec((ts, d), lambda bi, i: (bi * ns + i, 0)),
                   pl.BlockSpec((ts, LANES), lambda bi, i: (bi * ns + i, 0))],
        out_shape=[jax.ShapeDtypeStruct((b, s, d), _F32),
                   jax.ShapeDtypeStruct((b * s, d), _BF16),
                   jax.ShapeDtypeStruct((b * s, LANES), _F32)],
        scratch_shapes=[pltpu.VMEM((rows, d), _F32), pltpu.VMEM((rows, dg), _F32),
                        pltpu.VMEM((rows, dg), _F32)],
        compiler_params=_params(("parallel", "parallel")),
        name="pool_mixer",
    )(h, h, g.reshape(1, d), w_pool, scale.reshape(1, d), next_gain.reshape(1, d))


ATTN_HEADS_PER_STEP = 8
ATTN_Q_SCALE = _LOG2_E / math.sqrt(HEAD_DIM)


def _attn_kernel(q_ref, k_ref, v_ref, o_ref, carry_ref, acc_ref, *, tq, nhb):
    qi = pl.program_id(2)
    row = lax.broadcasted_iota(jnp.int32, (tq, tq), 0)
    col = lax.broadcasted_iota(jnp.int32, (tq, tq), 1)
    from_key = jnp.where(row >= col, 1.0, 0.0).astype(_BF16)
    from_key2 = jnp.concatenate([from_key, from_key], axis=0)
    causal = col < row
    heads = [slice(hh * HEAD_DIM, (hh + 1) * HEAD_DIM) for hh in range(nhb)]
    sign_bit = jnp.uint32(0x80000000)

    def step(starts, diagonal):
        zs = [[lax.dot_general(q_ref[:, cols], k_ref[pl.ds(st, tq), cols],
                               (((1,), (1,)), ((), ())), preferred_element_type=_F32)
               for cols in heads] for st in starts]
        tails = []
        for blk, per_head in enumerate(zs):
            tails.append([])
            for z in per_head:
                neg_abs = lax.bitcast_convert_type(
                    lax.bitcast_convert_type(z, jnp.uint32) | sign_bit, _F32)
                sp = jnp.maximum(z, 0.0) + jnp.log(1.0 + jnp.exp2(neg_abs)) * _LOG2_E
                if diagonal and blk == 0:
                    sp = jnp.where(causal, sp, 0.0)
                hi = sp.astype(_BF16)
                lo = (sp - hi.astype(_F32)).astype(_BF16)
                tails[blk].append(jnp.dot(jnp.concatenate([hi, lo], axis=1), from_key2,
                                          preferred_element_type=_F32))
        for hh, cols in enumerate(heads):
            carry = jnp.zeros((tq, LANES), _F32) if diagonal else carry_ref[hh]
            weights = []
            for blk in range(len(starts)):
                tail = tails[blk][hh]
                attn = jnp.exp2(zs[blk][hh] - jnp.concatenate([carry] * (tq // LANES), axis=1) - tail)
                if diagonal and blk == 0:
                    attn = jnp.where(causal, attn, 0.0)
                weights.append(attn.astype(_BF16))
                carry = carry + jnp.broadcast_to(tail[:, 0:1], carry.shape)
            carry_ref[hh] = carry
            attn_all = weights[0] if len(weights) == 1 else jnp.concatenate(weights[::-1], axis=1)
            av = jnp.dot(attn_all, v_ref[pl.ds(starts[-1], tq * len(starts)), cols],
                         preferred_element_type=_F32)
            acc_ref[hh] = av if diagonal else acc_ref[hh] + av

    def block_start(kb):
        return pl.multiple_of(kb * tq, tq)

    step([block_start(qi)], True)
    odd = qi % 2

    @pl.when(odd == 1)
    def _():
        step([block_start(qi - 1)], False)

    def body(j, _):
        later = qi - 1 - odd - 2 * j
        step([block_start(later), block_start(later - 1)], False)
        return 0

    lax.fori_loop(0, (qi - odd) // 2, body, 0)
    for hh, cols in enumerate(heads):
        o_ref[:, cols] = acc_ref[hh].astype(o_ref.dtype)


def _attention(qkv, b, s, d):
    nh = d // HEAD_DIM
    nhb = _tile(nh, ATTN_HEADS_PER_STEP)
    ng = nh // nhb
    tq = _tile(s, 256)
    nq = s // tq
    kernel = functools.partial(_attn_kernel, tq=tq, nhb=nhb)
    wide = nhb * HEAD_DIM
    return pl.pallas_call(
        kernel,
        grid=(b, ng, nq),
        in_specs=[
            pl.BlockSpec((tq, wide), lambda bi, gi, qi: (bi * nq + qi, gi)),
            pl.BlockSpec((s, wide), lambda bi, gi, qi: (bi, ng + gi)),
            pl.BlockSpec((s, wide), lambda bi, gi, qi: (bi, 2 * ng + gi)),
        ],
        out_specs=pl.BlockSpec((tq, wide), lambda bi, gi, qi: (bi * nq + qi, gi)),
        out_shape=jax.ShapeDtypeStruct((b * s, d), _BF16),
        scratch_shapes=[pltpu.VMEM((nhb, tq, LANES), _F32), pltpu.VMEM((nhb, tq, HEAD_DIM), _F32)],
        compiler_params=_params(("parallel", "parallel", "arbitrary")),
        name="stickbreak_attn",
    )(qkv, qkv, qkv)


def _ple_kernel(p_ref, w_ref, g_ref, o_ref):
    y = jnp.dot(p_ref[...].astype(_BF16), w_ref[...], preferred_element_type=_F32)
    o_ref[...] = _rms(y, g_ref[...])


def _ple_embed(p, layer, w_pe, g_pe):
    _, n, pd = p.shape
    d = w_pe.shape[-1]
    tm = _tile(n, 256)
    return pl.pallas_call(
        _ple_kernel,
        grid=(n // tm,),
        in_specs=[pl.BlockSpec((None, tm, pd), lambda i: (layer, i, 0)),
                  pl.BlockSpec((None, pd, d), lambda i: (layer, 0, 0)),
                  pl.BlockSpec((1, d), lambda i: (0, 0))],
        out_specs=pl.BlockSpec((tm, d), lambda i: (i, 0)),
        out_shape=jax.ShapeDtypeStruct((n, d), _F32),
        compiler_params=_params(("parallel",)),
        name="ple_embed",
    )(p, w_pe, g_pe.reshape(1, d))


def kernel(x, p, g_mix, g_mlp, w_pool, pool_scale, w_qkv, w_o, w_up, w_down,
           w_pe, g_pe, g_gate, w_gate, g_final):
    b, s, d = x.shape
    depth = g_mix.shape[0]
    n = b * s
    p2 = p.reshape(depth, n, p.shape[-1])
    w_pool_b, w_pe_b = w_pool.astype(_BF16), w_pe.astype(_BF16)
    w_up_b = w_up[0].astype(_BF16)
    w_qkv_b = w_o_b = None
    q_scale = jnp.concatenate([jnp.full((d,), ATTN_Q_SCALE, _F32), jnp.ones((2 * d,), _F32)])

    h = x.reshape(n, d)
    hg = sumsq = None
    for i in range(depth):
        if i % 2 == 0:
            h, hg, sumsq = _pool_layer(h.reshape(b, s, d), g_mix[i], w_pool_b, i // 2,
                                       pool_scale[i // 2], g_mlp[i])
            h = h.reshape(n, d)
        else:
            qkv = _matmul(hg, w_qkv_b, sumsq_in=sumsq, col_scale=q_scale, epilogue=_ep_identity,
                          out_dtype=_BF16, tm=1024, tn=1024, tk=4096, name="qkv_proj")
            o = _attention(qkv, b, s, d)
            h, hg, sumsq = _matmul(o, w_o_b, extras=(h,), epilogue=_ep_residual,
                                   out_dtype=_F32, next_gain=g_mlp[i],
                                   tm=1024, tn=512, tk=4096, name="attn_out_proj")
        feeds_attention = i + 1 < depth and (i + 1) % 2 == 1
        casts = [(w_down, i)]
        if feeds_attention:
            casts += [(w_qkv, (i + 1) // 2), (w_o, (i + 1) // 2)]
        outs = _matmul(hg, w_up_b, sumsq_in=sumsq, epilogue=_ep_relu2, out_dtype=_BF16,
                       casts=casts, tm=1024, tn=1024, tk=4096, name="mlp_up")
        a, w_down_b = outs[:2]
        if feeds_attention:
            w_qkv_b, w_o_b = outs[2:]
        casts = [(w_gate, i)] + ([(w_up, i + 1)] if i + 1 < depth else [])
        outs = _matmul(a, w_down_b, extras=(h,), epilogue=_ep_residual, out_dtype=_F32,
                       next_gain=g_gate[i], casts=casts,
                       tm=1024, tn=1024, tk=2048, name="mlp_down")
        h, hg, sumsq, w_gate_b = outs[:4]
        if i + 1 < depth:
            w_up_b = outs[4]
        e = _ple_embed(p2, i, w_pe_b, g_pe[i])
        out = _matmul(hg, w_gate_b, sumsq_in=sumsq, extras=(h, e), epilogue=_ep_gate,
                      out_dtype=_F32, next_gain=g_mix[i + 1] if feeds_attention else None,
                      tm=1024, tn=512, tk=4096, name="ple_gate")
        h, hg, sumsq = out if feeds_attention else (out, None, None)
    return _rmsnorm(h, g_final, _F32).reshape(b, s, d)
```

```python
import functools
import math

import jax
import jax.numpy as jnp
from jax import lax
from jax.experimental import pallas as pl
from jax.experimental.pallas import tpu as pltpu

HEAD_DIM = 128
POOL_WINDOWS = (2, 4, 8, 16)
POOL_HALO = 16
RMS_EPS = 1e-6
_LOG2_E = 1.0 / math.log(2.0)

LANES = 128
SUBLANES = 8
BF16_SUBLANES = 16
V7X_VMEM_BYTES = 64 * 1024 * 1024
VMEM_LIMIT_BYTES = V7X_VMEM_BYTES - 8 * 1024 * 1024

_BF16 = jnp.bfloat16
_F32 = jnp.float32


def _tile(dim, pref):
    t = min(dim, pref)
    while dim % t:
        t //= 2
    return t


def _params(semantics):
    return pltpu.CompilerParams(dimension_semantics=semantics, vmem_limit_bytes=VMEM_LIMIT_BYTES)


def _rms(x, g):
    ms = jnp.mean(x * x, axis=-1, keepdims=True)
    return x * lax.rsqrt(ms + RMS_EPS) * g


def _lane_group_sum(x):
    total = x[:, 0:LANES]
    for c in range(1, x.shape[1] // LANES):
        total = total + x[:, c * LANES:(c + 1) * LANES]
    return total


def _row_sumsq(h):
    part = _lane_group_sum(h * h)
    return jnp.broadcast_to(jnp.sum(part, axis=-1, keepdims=True), part.shape)


def _row_scale(sumsq_ref, inv_d, width):
    r = lax.rsqrt(sumsq_ref[...] * inv_d + RMS_EPS)
    return jnp.concatenate([r] * (width // LANES), axis=1) if width > LANES else r


def _rmsnorm_kernel(x_ref, g_ref, o_ref):
    o_ref[...] = _rms(x_ref[...], g_ref[...]).astype(o_ref.dtype)


def _rmsnorm(x, g, out_dtype):
    n, d = x.shape
    tm = _tile(n, 256)
    return pl.pallas_call(
        _rmsnorm_kernel,
        grid=(n // tm,),
        in_specs=[pl.BlockSpec((tm, d), lambda i: (i, 0)),
                  pl.BlockSpec((1, d), lambda i: (0, 0))],
        out_specs=pl.BlockSpec((tm, d), lambda i: (i, 0)),
        out_shape=jax.ShapeDtypeStruct((n, d), out_dtype),
        compiler_params=_params(("parallel",)),
        name="rmsnorm",
    )(x, g.reshape(1, d))


def _ep_relu2(acc):
    return jnp.square(jnp.maximum(acc, 0.0))


def _ep_identity(acc):
    return acc


def _ep_residual(acc, res):
    return res + acc


def _ep_gate(acc, res, e):
    return res + jax.nn.sigmoid(acc) * e


def _matmul_kernel(*refs, nk, n_extra, epilogue, on_residual, inv_d_in, col_scaled, inv_d_ple,
                   emit_norm, n_casts):
    refs = list(refs)
    a_ref, w_ref = refs[:2]
    pos = 2
    sumsq_in_ref = col_scale_ref = None
    if inv_d_in is not None:
        sumsq_in_ref = refs[pos]
        pos += 1
    if col_scaled:
        col_scale_ref = refs[pos]
        pos += 1
    extras = refs[pos:pos + n_extra]
    pos += n_extra
    ple_refs = None
    if inv_d_ple is not None:
        ple_refs = refs[pos:pos + 4]
        pos += 4
    gain_ref = hg_ref = sumsq_ref = None
    if emit_norm:
        gain_ref = refs[pos]
        pos += 1
    cast_srcs = refs[pos:pos + n_casts]
    pos += n_casts
    o_ref = refs[pos]
    pos += 1
    if emit_norm:
        hg_ref, sumsq_ref = refs[pos:pos + 2]
        pos += 2
    cast_dsts = refs[pos:pos + n_casts]
    pos += n_casts
    j = pl.program_id(1)

    def partial_product():
        for src, dst in zip(cast_srcs, cast_dsts):
            dst[...] = src[...].astype(dst.dtype)
        return jnp.dot(a_ref[...], w_ref[...], preferred_element_type=_F32)

    def emit(h_new):
        hg_ref[...] = (h_new * gain_ref[...]).astype(hg_ref.dtype)
        tile_sumsq = _row_sumsq(h_new)

        @pl.when(j == 0)
        def _():
            sumsq_ref[...] = tile_sumsq

        @pl.when(j > 0)
        def _():
            sumsq_ref[...] += tile_sumsq

    def embedding_tile():
        p_ref, w_pe_ref, g_pe_ref, ple_sumsq_ref = ple_refs
        y = jnp.dot(p_ref[...].astype(_BF16), w_pe_ref[...], preferred_element_type=_F32)
        return y * _row_scale(ple_sumsq_ref, inv_d_ple, y.shape[1]) * g_pe_ref[...]

    def finish(acc, *more):
        if sumsq_in_ref is not None:
            acc = acc * _row_scale(sumsq_in_ref, inv_d_in, acc.shape[1])
        if col_scale_ref is not None:
            acc = acc * col_scale_ref[...]
        out = epilogue(acc, *[e[...] for e in extras], *more)
        o_ref[...] = out.astype(o_ref.dtype)
        if emit_norm:
            emit(out)

    if nk == 1:
        more = (embedding_tile(),) if ple_refs is not None else ()
        finish(partial_product(), *more)
        return
    assert ple_refs is None
    k = pl.program_id(2)
    if on_residual:
        @pl.when(k == 0)
        def _():
            o_ref[...] = extras[0][...] + partial_product()

        @pl.when(jnp.logical_and(k > 0, k < nk - 1))
        def _():
            o_ref[...] += partial_product()

        @pl.when(k == nk - 1)
        def _():
            out = o_ref[...] + partial_product()
            o_ref[...] = out
            if emit_norm:
                emit(out)
        return
    acc_ref = refs[pos]

    @pl.when(k == 0)
    def _():
        acc_ref[...] = jnp.zeros_like(acc_ref)

    acc_ref[...] += partial_product()

    @pl.when(k == nk - 1)
    def _():
        finish(acc_ref[...])


def _cast_blocks(rows, n_steps):
    n_blocks = 1
    while (n_blocks * 2 <= n_steps and rows % (n_blocks * 2) == 0
           and rows // (n_blocks * 2) >= BF16_SUBLANES):
        n_blocks *= 2
    return rows // n_blocks, n_blocks


def _matmul(a, w, *, epilogue, out_dtype, tm, tn, tk, name,
            sumsq_in=None, col_scale=None, extras=(), ple=None, next_gain=None, casts=()):
    m, k = a.shape
    n = w.shape[-1]
    tm, tn, tk = _tile(m, tm), _tile(n, tn), _tile(k, tk)
    nj, nk = n // tn, k // tk
    n_steps = (m // tm) * nj * nk
    emit_norm = next_gain is not None
    on_residual = (nk > 1 and epilogue is _ep_residual and out_dtype == _F32
                   and sumsq_in is None and col_scale is None)
    tile_spec = pl.BlockSpec((tm, tn), lambda i, j, kk: (i, j))
    row_spec = pl.BlockSpec((tm, LANES), lambda i, j, kk: (i, 0))
    col_spec = pl.BlockSpec((1, tn), lambda i, j, kk: (0, j))
    in_specs = [pl.BlockSpec((tm, tk), lambda i, j, kk: (i, kk)),
                pl.BlockSpec((tk, tn), lambda i, j, kk: (kk, j))]
    operands = [a, w]
    if sumsq_in is not None:
        in_specs.append(row_spec)
        operands.append(sumsq_in)
    if col_scale is not None:
        in_specs.append(col_spec)
        operands.append(col_scale.reshape(1, n))
    in_specs += [tile_spec] * len(extras)
    operands += list(extras)
    if ple is not None:
        p, ple_layer, w_pe, g_pe, ple_sumsq = ple
        pd = p.shape[-1]
        in_specs += [pl.BlockSpec((None, tm, pd), lambda i, j, kk: (ple_layer, i, 0)),
                     pl.BlockSpec((None, pd, tn), lambda i, j, kk: (ple_layer, 0, j)),
                     col_spec, row_spec]
        operands += [p, w_pe, g_pe.reshape(1, n), ple_sumsq]
    out_specs = [tile_spec]
    out_shape = [jax.ShapeDtypeStruct((m, n), out_dtype)]
    if emit_norm:
        in_specs.append(col_spec)
        operands.append(next_gain.reshape(1, n))
        out_specs += [tile_spec, row_spec]
        out_shape += [jax.ShapeDtypeStruct((m, n), _BF16), jax.ShapeDtypeStruct((m, LANES), _F32)]
    for src, src_layer in casts:
        _, rows, cols = src.shape
        rb, n_blocks = _cast_blocks(rows, n_steps)

        def block_of(i, j, kk, n_blocks=n_blocks):
            return (((i * nj + j) * nk + kk) * n_blocks) // n_steps

        in_specs.append(pl.BlockSpec(
            (None, rb, cols), lambda i, j, kk, f=block_of, l=src_layer: (l, f(i, j, kk), 0)))
        operands.append(src)
        out_specs.append(pl.BlockSpec((rb, cols), lambda i, j, kk, f=block_of: (f(i, j, kk), 0)))
        out_shape.append(jax.ShapeDtypeStruct((rows, cols), _BF16))
    kernel = functools.partial(
        _matmul_kernel, nk=nk, n_extra=len(extras), epilogue=epilogue, on_residual=on_residual,
        inv_d_in=None if sumsq_in is None else 1.0 / k, col_scaled=col_scale is not None,
        inv_d_ple=None if ple is None else 1.0 / n, emit_norm=emit_norm, n_casts=len(casts))
    outs = pl.pallas_call(
        kernel,
        grid=(m // tm, nj, nk),
        in_specs=in_specs,
        out_specs=out_specs,
        out_shape=out_shape,
        scratch_shapes=[pltpu.VMEM((tm, tn), _F32)] if nk > 1 and not on_residual else [],
        compiler_params=_params(("arbitrary", "arbitrary", "arbitrary")),
        name=name,
    )(*operands)
    return outs[0] if len(outs) == 1 else tuple(outs)


def _pool_kernel(h_ref, halo_ref, g_ref, w_ref, scale_ref, gain_ref, o_ref, hg_ref, sumsq_ref,
                 xn_ref, lvl_a_ref, lvl_b_ref, *, ts, dg):
    i = pl.program_id(1)
    g = g_ref[...]
    h = h_ref[0]
    first = SUBLANES
    tile0 = first + POOL_HALO
    end = tile0 + ts
    xn_ref[0:first, :] = jnp.zeros((first, xn_ref.shape[1]), _F32)
    xn_ref[first:tile0, :] = jnp.where(i > 0, _rms(halo_ref[0], g), 0.0)
    xn_ref[tile0:, :] = _rms(h, g)
    for lvl_ref in (lvl_a_ref, lvl_b_ref):
        lvl_ref[0:first, :] = jnp.zeros((first, dg), _F32)
    pos = i * ts + lax.broadcasted_iota(jnp.int32, (ts, 1), 0)
    sumsq = None
    for gi, win in enumerate(POOL_WINDOWS):
        cols = slice(gi * dg, (gi + 1) * dg)
        cur = xn_ref[tile0:end, cols]
        src = lambda lo, hi, cols=cols: xn_ref[lo:hi, cols]
        width, levels = 1, [lvl_a_ref, lvl_b_ref]
        while 2 * width < win:
            dst = levels[0]
            dst[first:end, :] = src(first, end) + src(first - width, end - width)
            src = lambda lo, hi, dst=dst: dst[lo:hi, :]
            levels.reverse()
            width *= 2
        tot = src(tile0, end) + src(tile0 - width, end - width)
        cnt = jnp.minimum(pos + 1, win).astype(_F32)
        pooled = tot / cnt - cur
        y = jnp.dot(pooled.astype(_BF16), w_ref[gi], preferred_element_type=_F32)
        h_new = h[:, cols] + y * scale_ref[:, cols]
        o_ref[0, :, cols] = h_new
        hg_ref[:, cols] = (h_new * gain_ref[:, cols]).astype(hg_ref.dtype)
        part = _lane_group_sum(h_new * h_new)
        sumsq = part if sumsq is None else sumsq + part
    sumsq_ref[...] = jnp.broadcast_to(jnp.sum(sumsq, axis=-1, keepdims=True), sumsq.shape)


def _pool_layer(h, g, w_pool, layer, scale, next_gain):
    b, s, d = h.shape
    ng = len(POOL_WINDOWS)
    dg = d // ng
    ts = _tile(s, 256)
    ns = s // ts
    halo_blocks = ts // POOL_HALO
    rows = SUBLANES + POOL_HALO + ts
    kernel = functools.partial(_pool_kernel, ts=ts, dg=dg)
    vec_spec = pl.BlockSpec((1, d), lambda bi, i: (0, 0))
    return pl.pallas_call(
        kernel,
        grid=(b, ns),
        in_specs=[
            pl.BlockSpec((1, ts, d), lambda bi, i: (bi, i, 0)),
            pl.BlockSpec((1, POOL_HALO, d),
                         lambda bi, i: (bi, jnp.maximum(i * halo_blocks - 1, 0), 0)),
            vec_spec,
            pl.BlockSpec((None, ng, dg, dg), lambda bi, i: (layer, 0, 0, 0)),
            vec_spec,
            vec_spec,
        ],
        out_specs=[pl.BlockSpec((1, ts, d), lambda bi, i: (bi, i, 0)),
                   pl.BlockSpec((ts, d), lambda bi, i: (bi * ns + i, 0)),
                   pl.BlockSpec((ts, LANES), lambda bi, i: (bi * ns + i, 0))],
        out_shape=[jax.ShapeDtypeStruct((b, s, d), _F32),
                   jax.ShapeDtypeStruct((b * s, d), _BF16),
                   jax.ShapeDtypeStruct((b * s, LANES), _F32)],
        scratch_shapes=[pltpu.VMEM((rows, d), _F32), pltpu.VMEM((rows, dg), _F32),
                        pltpu.VMEM((rows, dg), _F32)],
        compiler_params=_params(("parallel", "parallel")),
        name="pool_mixer",
    )(h, h, g.reshape(1, d), w_pool, scale.reshape(1, d), next_gain.reshape(1, d))


ATTN_HEADS_PER_STEP = 8
ATTN_Q_SCALE = _LOG2_E / math.sqrt(HEAD_DIM)


def _attn_kernel(q_ref, k_ref, v_ref, o_ref, carry_ref, acc_ref, *, tq, nhb):
    qi = pl.program_id(2)
    row = lax.broadcasted_iota(jnp.int32, (tq, tq), 0)
    col = lax.broadcasted_iota(jnp.int32, (tq, tq), 1)
    from_key = jnp.where(row >= col, 1.0, 0.0).astype(_BF16)
    from_key2 = jnp.concatenate([from_key, from_key], axis=0)
    causal = col < row
    heads = [slice(hh * HEAD_DIM, (hh + 1) * HEAD_DIM) for hh in range(nhb)]
    sign_bit = jnp.uint32(0x80000000)

    def step(starts, diagonal):
        zs = [[lax.dot_general(q_ref[:, cols], k_ref[pl.ds(st, tq), cols],
                               (((1,), (1,)), ((), ())), preferred_element_type=_F32)
               for cols in heads] for st in starts]
        tails = []
        for blk, per_head in enumerate(zs):
            tails.append([])
            for z in per_head:
                neg_abs = lax.bitcast_convert_type(
                    lax.bitcast_convert_type(z, jnp.uint32) | sign_bit, _F32)
                sp = jnp.maximum(z, 0.0) + jnp.log(1.0 + jnp.exp2(neg_abs)) * _LOG2_E
                if diagonal and blk == 0:
                    sp = jnp.where(causal, sp, 0.0)
                hi = sp.astype(_BF16)
                lo = (sp - hi.astype(_F32)).astype(_BF16)
                tails[blk].append(jnp.dot(jnp.concatenate([hi, lo], axis=1), from_key2,
                                          preferred_element_type=_F32))
        for hh, cols in enumerate(heads):
            carry = jnp.zeros((tq, LANES), _F32) if diagonal else carry_ref[hh]
            weights = []
            for blk in range(len(starts)):
                tail = tails[blk][hh]
                attn = jnp.exp2(zs[blk][hh] - jnp.concatenate([carry] * (tq // LANES), axis=1) - tail)
                if diagonal and blk == 0:
                    attn = jnp.where(causal, attn, 0.0)
                weights.append(attn.astype(_BF16))
                carry = carry + jnp.broadcast_to(tail[:, 0:1], carry.shape)
            carry_ref[hh] = carry
            attn_all = weights[0] if len(weights) == 1 else jnp.concatenate(weights[::-1], axis=1)
            av = jnp.dot(attn_all, v_ref[pl.ds(starts[-1], tq * len(starts)), cols],
                         preferred_element_type=_F32)
            acc_ref[hh] = av if diagonal else acc_ref[hh] + av

    def block_start(kb):
        return pl.multiple_of(kb * tq, tq)

    step([block_start(qi)], True)
    odd = qi % 2

    @pl.when(odd == 1)
    def _():
        step([block_start(qi - 1)], False)

    def body(j, _):
        later = qi - 1 - odd - 2 * j
        step([block_start(later), block_start(later - 1)], False)
        return 0

    lax.fori_loop(0, (qi - odd) // 2, body, 0)
    for hh, cols in enumerate(heads):
        o_ref[:, cols] = acc_ref[hh].astype(o_ref.dtype)


def _attention(qkv, b, s, d):
    nh = d // HEAD_DIM
    nhb = _tile(nh, ATTN_HEADS_PER_STEP)
    ng = nh // nhb
    tq = _tile(s, 256)
    nq = s // tq
    kernel = functools.partial(_attn_kernel, tq=tq, nhb=nhb)
    wide = nhb * HEAD_DIM
    return pl.pallas_call(
        kernel,
        grid=(b, ng, nq),
        in_specs=[
            pl.BlockSpec((tq, wide), lambda bi, gi, qi: (bi * nq + qi, gi)),
            pl.BlockSpec((s, wide), lambda bi, gi, qi: (bi, ng + gi)),
            pl.BlockSpec((s, wide), lambda bi, gi, qi: (bi, 2 * ng + gi)),
        ],
        out_specs=pl.BlockSpec((tq, wide), lambda bi, gi, qi: (bi * nq + qi, gi)),
        out_shape=jax.ShapeDtypeStruct((b * s, d), _BF16),
        scratch_shapes=[pltpu.VMEM((nhb, tq, LANES), _F32), pltpu.VMEM((nhb, tq, HEAD_DIM), _F32)],
        compiler_params=_params(("parallel", "parallel", "arbitrary")),
        name="stickbreak_attn",
    )(qkv, qkv, qkv)


def _ple_sumsq_kernel(p_ref, w_ref, o_ref):
    y = jnp.dot(p_ref[...].astype(_BF16), w_ref[...], preferred_element_type=_F32)
    o_ref[...] = _row_sumsq(y)


def _ple_sumsq(p, layer, w_pe):
    _, n, pd = p.shape
    d = w_pe.shape[-1]
    tm = _tile(n, 512)
    return pl.pallas_call(
        _ple_sumsq_kernel,
        grid=(n // tm,),
        in_specs=[pl.BlockSpec((None, tm, pd), lambda i: (layer, i, 0)),
                  pl.BlockSpec((None, pd, d), lambda i: (layer, 0, 0))],
        out_specs=pl.BlockSpec((tm, LANES), lambda i: (i, 0)),
        out_shape=jax.ShapeDtypeStruct((n, LANES), _F32),
        compiler_params=_params(("parallel",)),
        name="ple_sumsq",
    )(p, w_pe)


def kernel(x, p, g_mix, g_mlp, w_pool, pool_scale, w_qkv, w_o, w_up, w_down,
           w_pe, g_pe, g_gate, w_gate, g_final):
    b, s, d = x.shape
    depth = g_mix.shape[0]
    n = b * s
    p2 = p.reshape(depth, n, p.shape[-1])
    w_pool_b, w_pe_b = w_pool.astype(_BF16), w_pe.astype(_BF16)
    w_up_b = w_up[0].astype(_BF16)
    w_qkv_b = w_o_b = None
    q_scale = jnp.concatenate([jnp.full((d,), ATTN_Q_SCALE, _F32), jnp.ones((2 * d,), _F32)])

    h = x.reshape(n, d)
    hg = sumsq = None
    for i in range(depth):
        if i % 2 == 0:
            h, hg, sumsq = _pool_layer(h.reshape(b, s, d), g_mix[i], w_pool_b, i // 2,
                                       pool_scale[i // 2], g_mlp[i])
            h = h.reshape(n, d)
        else:
            qkv = _matmul(hg, w_qkv_b, sumsq_in=sumsq, col_scale=q_scale, epilogue=_ep_identity,
                          out_dtype=_BF16, tm=1024, tn=1024, tk=4096, name="qkv_proj")
            o = _attention(qkv, b, s, d)
            h, hg, sumsq = _matmul(o, w_o_b, extras=(h,), epilogue=_ep_residual,
                                   out_dtype=_F32, next_gain=g_mlp[i],
                                   tm=1024, tn=512, tk=4096, name="attn_out_proj")
        feeds_attention = i + 1 < depth and (i + 1) % 2 == 1
        casts = [(w_down, i)]
        if feeds_attention:
            casts += [(w_qkv, (i + 1) // 2), (w_o, (i + 1) // 2)]
        outs = _matmul(hg, w_up_b, sumsq_in=sumsq, epilogue=_ep_relu2, out_dtype=_BF16,
                       casts=casts, tm=1024, tn=1024, tk=4096, name="mlp_up")
        a, w_down_b = outs[:2]
        if feeds_attention:
            w_qkv_b, w_o_b = outs[2:]
        casts = [(w_gate, i)] + ([(w_up, i + 1)] if i + 1 < depth else [])
        outs = _matmul(a, w_down_b, extras=(h,), epilogue=_ep_residual, out_dtype=_F32,
                       next_gain=g_gate[i], casts=casts,
                       tm=1024, tn=1024, tk=2048, name="mlp_down")
        h, hg, sumsq, w_gate_b = outs[:4]
        if i + 1 < depth:
            w_up_b = outs[4]
        ple = (p2, i, w_pe_b, g_pe[i], _ple_sumsq(p2, i, w_pe_b))
        out = _matmul(hg, w_gate_b, sumsq_in=sumsq, extras=(h,), ple=ple, epilogue=_ep_gate,
                      out_dtype=_F32, next_gain=g_mix[i + 1] if feeds_attention else None,
                      tm=1024, tn=512, tk=4096, name="ple_gate")
        h, hg, sumsq = out if feeds_attention else (out, None, None)
    return _rmsnorm(h, g_final, _F32).reshape(b, s, d)
```

```python
import functools
import math

import jax
import jax.numpy as jnp
from jax import lax
from jax.experimental import pallas as pl
from jax.experimental.pallas import tpu as pltpu

HEAD_DIM = 128
POOL_WINDOWS = (2, 4, 8, 16)
POOL_HALO = 16
RMS_EPS = 1e-6
_LOG2_E = 1.0 / math.log(2.0)

LANES = 128
SUBLANES = 8
BF16_SUBLANES = 16
V7X_VMEM_BYTES = 64 * 1024 * 1024
VMEM_LIMIT_BYTES = V7X_VMEM_BYTES - 8 * 1024 * 1024

_BF16 = jnp.bfloat16
_F32 = jnp.float32


def _tile(dim, pref):
    t = min(dim, pref)
    while dim % t:
        t //= 2
    return t


def _params(semantics):
    return pltpu.CompilerParams(dimension_semantics=semantics, vmem_limit_bytes=VMEM_LIMIT_BYTES)


def _rms(x, g):
    ms = jnp.mean(x * x, axis=-1, keepdims=True)
    return x * lax.rsqrt(ms + RMS_EPS) * g


def _lane_group_sum(x):
    total = x[:, 0:LANES]
    for c in range(1, x.shape[1] // LANES):
        total = total + x[:, c * LANES:(c + 1) * LANES]
    return total


def _row_sumsq(h):
    part = _lane_group_sum(h * h)
    return jnp.broadcast_to(jnp.sum(part, axis=-1, keepdims=True), part.shape)


def _row_scale(sumsq_ref, inv_d, width):
    r = lax.rsqrt(sumsq_ref[...] * inv_d + RMS_EPS)
    return jnp.concatenate([r] * (width // LANES), axis=1) if width > LANES else r


def _rmsnorm_kernel(x_ref, g_ref, o_ref):
    o_ref[...] = _rms(x_ref[...], g_ref[...]).astype(o_ref.dtype)


def _rmsnorm(x, g, out_dtype):
    n, d = x.shape
    tm = _tile(n, 256)
    return pl.pallas_call(
        _rmsnorm_kernel,
        grid=(n // tm,),
        in_specs=[pl.BlockSpec((tm, d), lambda i: (i, 0)),
                  pl.BlockSpec((1, d), lambda i: (0, 0))],
        out_specs=pl.BlockSpec((tm, d), lambda i: (i, 0)),
        out_shape=jax.ShapeDtypeStruct((n, d), out_dtype),
        compiler_params=_params(("parallel",)),
        name="rmsnorm",
    )(x, g.reshape(1, d))


def _ep_relu2(acc):
    return jnp.square(jnp.maximum(acc, 0.0))


def _ep_identity(acc):
    return acc


def _ep_residual(acc, res):
    return res + acc


def _ep_gate(acc, res, e):
    return res + jax.nn.sigmoid(acc) * e


def _matmul_kernel(*refs, nk, n_extra, epilogue, on_residual, inv_d_in, col_scaled, inv_d_ple,
                   emit_norm, n_casts):
    refs = list(refs)
    a_ref, w_ref = refs[:2]
    pos = 2
    sumsq_in_ref = col_scale_ref = None
    if inv_d_in is not None:
        sumsq_in_ref = refs[pos]
        pos += 1
    if col_scaled:
        col_scale_ref = refs[pos]
        pos += 1
    extras = refs[pos:pos + n_extra]
    pos += n_extra
    ple_refs = None
    if inv_d_ple is not None:
        ple_refs = refs[pos:pos + 4]
        pos += 4
    gain_ref = hg_ref = sumsq_ref = None
    if emit_norm:
        gain_ref = refs[pos]
        pos += 1
    cast_srcs = refs[pos:pos + n_casts]
    pos += n_casts
    o_ref = refs[pos]
    pos += 1
    if emit_norm:
        hg_ref, sumsq_ref = refs[pos:pos + 2]
        pos += 2
    cast_dsts = refs[pos:pos + n_casts]
    pos += n_casts
    j = pl.program_id(1)

    def partial_product():
        for src, dst in zip(cast_srcs, cast_dsts):
            dst[...] = src[...].astype(dst.dtype)
        return jnp.dot(a_ref[...], w_ref[...], preferred_element_type=_F32)

    def emit(h_new):
        hg_ref[...] = (h_new * gain_ref[...]).astype(hg_ref.dtype)
        tile_sumsq = _row_sumsq(h_new)

        @pl.when(j == 0)
        def _():
            sumsq_ref[...] = tile_sumsq

        @pl.when(j > 0)
        def _():
            sumsq_ref[...] += tile_sumsq

    def embedding_tile():
        p_ref, w_pe_ref, g_pe_ref, ple_sumsq_ref = ple_refs
        y = jnp.dot(p_ref[...].astype(_BF16), w_pe_ref[...], preferred_element_type=_F32)
        return y * _row_scale(ple_sumsq_ref, inv_d_ple, y.shape[1]) * g_pe_ref[...]

    def finish(acc, *more):
        if sumsq_in_ref is not None:
            acc = acc * _row_scale(sumsq_in_ref, inv_d_in, acc.shape[1])
        if col_scale_ref is not None:
            acc = acc * col_scale_ref[...]
        out = epilogue(acc, *[e[...] for e in extras], *more)
        o_ref[...] = out.astype(o_ref.dtype)
        if emit_norm:
            emit(out)

    if nk == 1:
        more = (embedding_tile(),) if ple_refs is not None else ()
        finish(partial_product(), *more)
        return
    assert ple_refs is None
    k = pl.program_id(2)
    if on_residual:
        @pl.when(k == 0)
        def _():
            o_ref[...] = extras[0][...] + partial_product()

        @pl.when(jnp.logical_and(k > 0, k < nk - 1))
        def _():
            o_ref[...] += partial_product()

        @pl.when(k == nk - 1)
        def _():
            out = o_ref[...] + partial_product()
            o_ref[...] = out
            if emit_norm:
                emit(out)
        return
    acc_ref = refs[pos]

    @pl.when(k == 0)
    def _():
        acc_ref[...] = jnp.zeros_like(acc_ref)

    acc_ref[...] += partial_product()

    @pl.when(k == nk - 1)
    def _():
        finish(acc_ref[...])


def _cast_blocks(rows, n_steps):
    n_blocks = 1
    while (n_blocks * 2 <= n_steps and rows % (n_blocks * 2) == 0
           and rows // (n_blocks * 2) >= BF16_SUBLANES):
        n_blocks *= 2
    return rows // n_blocks, n_blocks


def _matmul(a, w, *, epilogue, out_dtype, tm, tn, tk, name,
            sumsq_in=None, col_scale=None, extras=(), ple=None, next_gain=None, casts=()):
    m, k = a.shape
    n = w.shape[-1]
    tm, tn, tk = _tile(m, tm), _tile(n, tn), _tile(k, tk)
    nj, nk = n // tn, k // tk
    n_steps = (m // tm) * nj * nk
    emit_norm = next_gain is not None
    on_residual = (nk > 1 and epilogue is _ep_residual and out_dtype == _F32
                   and sumsq_in is None and col_scale is None)
    tile_spec = pl.BlockSpec((tm, tn), lambda i, j, kk: (i, j))
    row_spec = pl.BlockSpec((tm, LANES), lambda i, j, kk: (i, 0))
    col_spec = pl.BlockSpec((1, tn), lambda i, j, kk: (0, j))
    in_specs = [pl.BlockSpec((tm, tk), lambda i, j, kk: (i, kk)),
                pl.BlockSpec((tk, tn), lambda i, j, kk: (kk, j))]
    operands = [a, w]
    if sumsq_in is not None:
        in_specs.append(row_spec)
        operands.append(sumsq_in)
    if col_scale is not None:
        in_specs.append(col_spec)
        operands.append(col_scale.reshape(1, n))
    in_specs += [tile_spec] * len(extras)
    operands += list(extras)
    if ple is not None:
        p, ple_layer, w_pe, g_pe, ple_sumsq = ple
        pd = p.shape[-1]
        in_specs += [pl.BlockSpec((None, tm, pd), lambda i, j, kk: (ple_layer, i, 0)),
                     pl.BlockSpec((None, pd, tn), lambda i, j, kk: (ple_layer, 0, j)),
                     col_spec, row_spec]
        operands += [p, w_pe, g_pe.reshape(1, n), ple_sumsq]
    out_specs = [tile_spec]
    out_shape = [jax.ShapeDtypeStruct((m, n), out_dtype)]
    if emit_norm:
        in_specs.append(col_spec)
        operands.append(next_gain.reshape(1, n))
        out_specs += [tile_spec, row_spec]
        out_shape += [jax.ShapeDtypeStruct((m, n), _BF16), jax.ShapeDtypeStruct((m, LANES), _F32)]
    for src, src_layer in casts:
        _, rows, cols = src.shape
        rb, n_blocks = _cast_blocks(rows, n_steps)

        def block_of(i, j, kk, n_blocks=n_blocks):
            return (((i * nj + j) * nk + kk) * n_blocks) // n_steps

        in_specs.append(pl.BlockSpec(
            (None, rb, cols), lambda i, j, kk, f=block_of, l=src_layer: (l, f(i, j, kk), 0)))
        operands.append(src)
        out_specs.append(pl.BlockSpec((rb, cols), lambda i, j, kk, f=block_of: (f(i, j, kk), 0)))
        out_shape.append(jax.ShapeDtypeStruct((rows, cols), _BF16))
    kernel = functools.partial(
        _matmul_kernel, nk=nk, n_extra=len(extras), epilogue=epilogue, on_residual=on_residual,
        inv_d_in=None if sumsq_in is None else 1.0 / k, col_scaled=col_scale is not None,
        inv_d_ple=None if ple is None else 1.0 / n, emit_norm=emit_norm, n_casts=len(casts))
    outs = pl.pallas_call(
        kernel,
        grid=(m // tm, nj, nk),
        in_specs=in_specs,
        out_specs=out_specs,
        out_shape=out_shape,
        scratch_shapes=[pltpu.VMEM((tm, tn), _F32)] if nk > 1 and not on_residual else [],
        compiler_params=_params(("arbitrary", "arbitrary", "arbitrary")),
        name=name,
    )(*operands)
    return outs[0] if len(outs) == 1 else tuple(outs)


def _pool_kernel(h_ref, halo_ref, g_ref, w_ref, scale_ref, gain_ref, o_ref, hg_ref, sumsq_ref,
                 xn_ref, lvl_a_ref, lvl_b_ref, *, ts, dg):
    i = pl.program_id(1)
    g = g_ref[...]
    h = h_ref[0]
    first = SUBLANES
    tile0 = first + POOL_HALO
    end = tile0 + ts
    xn_ref[0:first, :] = jnp.zeros((first, xn_ref.shape[1]), _F32)
    xn_ref[first:tile0, :] = jnp.where(i > 0, _rms(halo_ref[0], g), 0.0)
    xn_ref[tile0:, :] = _rms(h, g)
    for lvl_ref in (lvl_a_ref, lvl_b_ref):
        lvl_ref[0:first, :] = jnp.zeros((first, dg), _F32)
    pos = i * ts + lax.broadcasted_iota(jnp.int32, (ts, 1), 0)
    sumsq = None
    for gi, win in enumerate(POOL_WINDOWS):
        cols = slice(gi * dg, (gi + 1) * dg)
        cur = xn_ref[tile0:end, cols]
        src = lambda lo, hi, cols=cols: xn_ref[lo:hi, cols]
        width, levels = 1, [lvl_a_ref, lvl_b_ref]
        while 2 * width < win:
            dst = levels[0]
            dst[first:end, :] = src(first, end) + src(first - width, end - width)
            src = lambda lo, hi, dst=dst: dst[lo:hi, :]
            levels.reverse()
            width *= 2
        tot = src(tile0, end) + src(tile0 - width, end - width)
        cnt = jnp.minimum(pos + 1, win).astype(_F32)
        pooled = tot / cnt - cur
        y = jnp.dot(pooled.astype(_BF16), w_ref[gi], preferred_element_type=_F32)
        h_new = h[:, cols] + y * scale_ref[:, cols]
        o_ref[0, :, cols] = h_new
        hg_ref[:, cols] = (h_new * gain_ref[:, cols]).astype(hg_ref.dtype)
        part = _lane_group_sum(h_new * h_new)
        sumsq = part if sumsq is None else sumsq + part
    sumsq_ref[...] = jnp.broadcast_to(jnp.sum(sumsq, axis=-1, keepdims=True), sumsq.shape)


def _pool_layer(h, g, w_pool, layer, scale, next_gain):
    b, s, d = h.shape
    ng = len(POOL_WINDOWS)
    dg = d // ng
    ts = _tile(s, 256)
    ns = s // ts
    halo_blocks = ts // POOL_HALO
    rows = SUBLANES + POOL_HALO + ts
    kernel = functools.partial(_pool_kernel, ts=ts, dg=dg)
    vec_spec = pl.BlockSpec((1, d), lambda bi, i: (0, 0))
    return pl.pallas_call(
        kernel,
        grid=(b, ns),
        in_specs=[
            pl.BlockSpec((1, ts, d), lambda bi, i: (bi, i, 0)),
            pl.BlockSpec((1, POOL_HALO, d),
                         lambda bi, i: (bi, jnp.maximum(i * halo_blocks - 1, 0), 0)),
            vec_spec,
            pl.BlockSpec((None, ng, dg, dg), lambda bi, i: (layer, 0, 0, 0)),
            vec_spec,
            vec_spec,
        ],
        out_specs=[pl.BlockSpec((1, ts, d), lambda bi, i: (bi, i, 0)),
                   pl.BlockSpec((ts, d), lambda bi, i: (bi * ns + i, 0)),
                   pl.BlockSpec((ts, LANES), lambda bi, i: (bi * ns + i, 0))],
        out_shape=[jax.ShapeDtypeStruct((b, s, d), _F32),
                   jax.ShapeDtypeStruct((b * s, d), _BF16),
                   jax.ShapeDtypeStruct((b * s, LANES), _F32)],
        scratch_shapes=[pltpu.VMEM((rows, d), _F32), pltpu.VMEM((rows, dg), _F32),
                        pltpu.VMEM((rows, dg), _F32)],
        compiler_params=_params(("parallel", "parallel")),
        name="pool_mixer",
    )(h, h, g.reshape(1, d), w_pool, scale.reshape(1, d), next_gain.reshape(1, d))


ATTN_HEADS_PER_STEP = 8
ATTN_Q_SCALE = _LOG2_E / math.sqrt(HEAD_DIM)


def _attn_kernel(q_ref, k_ref, v_ref, o_ref, carry_ref, acc_ref, *, tq, nhb):
    qi = pl.program_id(2)
    row = lax.broadcasted_iota(jnp.int32, (tq, tq), 0)
    col = lax.broadcasted_iota(jnp.int32, (tq, tq), 1)
    from_key = jnp.where(row >= col, 1.0, 0.0).astype(_BF16)
    from_key2 = jnp.concatenate([from_key, from_key], axis=0)
    causal = col < row
    heads = [slice(hh * HEAD_DIM, (hh + 1) * HEAD_DIM) for hh in range(nhb)]
    sign_bit = jnp.uint32(0x80000000)

    def step(starts, diagonal):
        zs = [[lax.dot_general(q_ref[:, cols], k_ref[pl.ds(st, tq), cols],
                               (((1,), (1,)), ((), ())), preferred_element_type=_F32)
               for cols in heads] for st in starts]
        tails = []
        for blk, per_head in enumerate(zs):
            tails.append([])
            for z in per_head:
                neg_abs = lax.bitcast_convert_type(
                    lax.bitcast_convert_type(z, jnp.uint32) | sign_bit, _F32)
                sp = jnp.maximum(z, 0.0) + jnp.log(1.0 + jnp.exp2(neg_abs)) * _LOG2_E
                if diagonal and blk == 0:
                    sp = jnp.where(causal, sp, 0.0)
                hi = sp.astype(_BF16)
                lo = (sp - hi.astype(_F32)).astype(_BF16)
                tails[blk].append(jnp.dot(jnp.concatenate([hi, lo], axis=1), from_key2,
                                          preferred_element_type=_F32))
        for hh, cols in enumerate(heads):
            carry = jnp.zeros((tq, LANES), _F32) if diagonal else carry_ref[hh]
            weights = []
            for blk in range(len(starts)):
                tail = tails[blk][hh]
                attn = jnp.exp2(zs[blk][hh] - jnp.concatenate([carry] * (tq // LANES), axis=1) - tail)
                if diagonal and blk == 0:
                    attn = jnp.where(causal, attn, 0.0)
                weights.append(attn.astype(_BF16))
                carry = carry + jnp.broadcast_to(tail[:, 0:1], carry.shape)
            carry_ref[hh] = carry
            attn_all = weights[0] if len(weights) == 1 else jnp.concatenate(weights[::-1], axis=1)
            av = jnp.dot(attn_all, v_ref[pl.ds(starts[-1], tq * len(starts)), cols],
                         preferred_element_type=_F32)
            acc_ref[hh] = av if diagonal else acc_ref[hh] + av

    def block_start(kb):
        return pl.multiple_of(kb * tq, tq)

    step([block_start(qi)], True)
    odd = qi % 2

    @pl.when(odd == 1)
    def _():
        step([block_start(qi - 1)], False)

    def body(j, _):
        later = qi - 1 - odd - 2 * j
        step([block_start(later), block_start(later - 1)], False)
        return 0

    lax.fori_loop(0, (qi - odd) // 2, body, 0)
    for hh, cols in enumerate(heads):
        o_ref[:, cols] = acc_ref[hh].astype(o_ref.dtype)


def _attention(qkv, b, s, d):
    nh = d // HEAD_DIM
    nhb = _tile(nh, ATTN_HEADS_PER_STEP)
    ng = nh // nhb
    tq = _tile(s, 256)
    nq = s // tq
    kernel = functools.partial(_attn_kernel, tq=tq, nhb=nhb)
    wide = nhb * HEAD_DIM
    return pl.pallas_call(
        kernel,
        grid=(b, ng, nq),
        in_specs=[
            pl.BlockSpec((tq, wide), lambda bi, gi, qi: (bi * nq + qi, gi)),
            pl.BlockSpec((s, wide), lambda bi, gi, qi: (bi, ng + gi)),
            pl.BlockSpec((s, wide), lambda bi, gi, qi: (bi, 2 * ng + gi)),
        ],
        out_specs=pl.BlockSpec((tq, wide), lambda bi, gi, qi: (bi * nq + qi, gi)),
        out_shape=jax.ShapeDtypeStruct((b * s, d), _BF16),
        scratch_shapes=[pltpu.VMEM((nhb, tq, LANES), _F32), pltpu.VMEM((nhb, tq, HEAD_DIM), _F32)],
        compiler_params=_params(("parallel", "parallel", "arbitrary")),
        name="stickbreak_attn",
    )(qkv, qkv, qkv)


def _ple_sumsq_kernel(p_ref, w_ref, o_ref):
    y = jnp.dot(p_ref[...].astype(_BF16), w_ref[...], preferred_element_type=_F32)
    o_ref[...] = _row_sumsq(y)


def _ple_sumsq(p, layer, w_pe):
    _, n, pd = p.shape
    d = w_pe.shape[-1]
    tm = _tile(n, 512)
    return pl.pallas_call(
        _ple_sumsq_kernel,
        grid=(n // tm,),
        in_specs=[pl.BlockSpec((None, tm, pd), lambda i: (layer, i, 0)),
                  pl.BlockSpec((None, pd, d), lambda i: (layer, 0, 0))],
        out_specs=pl.BlockSpec((tm, LANES), lambda i: (i, 0)),
        out_shape=jax.ShapeDtypeStruct((n, LANES), _F32),
        compiler_params=_params(("parallel",)),
        name="ple_sumsq",
    )(p, w_pe)


def kernel(x, p, g_mix, g_mlp, w_pool, pool_scale, w_qkv, w_o, w_up, w_down,
           w_pe, g_pe, g_gate, w_gate, g_final):
    b, s, d = x.shape
    depth = g_mix.shape[0]
    n = b * s
    p2 = p.reshape(depth, n, p.shape[-1])
    w_pool_b, w_pe_b = w_pool.astype(_BF16), w_pe.astype(_BF16)
    w_up_b = w_up[0].astype(_BF16)
    w_qkv_b = w_o_b = None
    q_scale = jnp.concatenate([jnp.full((d,), ATTN_Q_SCALE, _F32), jnp.ones((2 * d,), _F32)])

    h = x.reshape(n, d)
    hg = sumsq = None
    for i in range(depth):
        if i % 2 == 0:
            h, hg, sumsq = _pool_layer(h.reshape(b, s, d), g_mix[i], w_pool_b, i // 2,
                                       pool_scale[i // 2], g_mlp[i])
            h = h.reshape(n, d)
        else:
            qkv, w_up_b = _matmul(hg, w_qkv_b, sumsq_in=sumsq, col_scale=q_scale,
                                  epilogue=_ep_identity, out_dtype=_BF16, casts=[(w_up, i)],
                                  tm=1024, tn=1024, tk=4096, name="qkv_proj")
            o = _attention(qkv, b, s, d)
            h, hg, sumsq = _matmul(o, w_o_b, extras=(h,), epilogue=_ep_residual,
                                   out_dtype=_F32, next_gain=g_mlp[i],
                                   tm=1024, tn=512, tk=4096, name="attn_out_proj")
        feeds_attention = i + 1 < depth and (i + 1) % 2 == 1
        feeds_pool = i + 1 < depth and not feeds_attention
        casts = [(w_down, i), (w_gate, i)]
        if feeds_attention:
            casts += [(w_qkv, (i + 1) // 2), (w_o, (i + 1) // 2)]
        outs = _matmul(hg, w_up_b, sumsq_in=sumsq, epilogue=_ep_relu2, out_dtype=_BF16,
                       casts=casts, tm=1024, tn=1024, tk=4096, name="mlp_up")
        a, w_down_b, w_gate_b = outs[:3]
        if feeds_attention:
            w_qkv_b, w_o_b = outs[3:]
        h, hg, sumsq = _matmul(a, w_down_b, extras=(h,), epilogue=_ep_residual, out_dtype=_F32,
                               next_gain=g_gate[i], tm=1024, tn=1024, tk=2048, name="mlp_down")
        ple = (p2, i, w_pe_b, g_pe[i], _ple_sumsq(p2, i, w_pe_b))
        outs = _matmul(hg, w_gate_b, sumsq_in=sumsq, extras=(h,), ple=ple, epilogue=_ep_gate,
                       out_dtype=_F32, next_gain=g_mix[i + 1] if feeds_attention else None,
                       casts=[(w_up, i + 1)] if feeds_pool else [],
                       tm=1024, tn=512, tk=4096, name="ple_gate")
        hg = sumsq = None
        if feeds_attention:
            h, hg, sumsq = outs
        elif feeds_pool:
            h, w_up_b = outs
        else:
            h = outs
    return _rmsnorm(h, g_final, _F32).reshape(b, s, d)
```

```python
import functools
import math

import jax
import jax.numpy as jnp
from jax import lax
from jax.experimental import pallas as pl
from jax.experimental.pallas import tpu as pltpu

HEAD_DIM = 128
POOL_WINDOWS = (2, 4, 8, 16)
POOL_HALO = 16
RMS_EPS = 1e-6
_LOG2_E = 1.0 / math.log(2.0)

LANES = 128
SUBLANES = 8
BF16_SUBLANES = 16
V7X_VMEM_BYTES = 64 * 1024 * 1024
VMEM_LIMIT_BYTES = V7X_VMEM_BYTES - 8 * 1024 * 1024

_BF16 = jnp.bfloat16
_F32 = jnp.float32


def _tile(dim, pref):
    t = min(dim, pref)
    while dim % t:
        t //= 2
    return t


def _params(semantics, vmem_limit_bytes=VMEM_LIMIT_BYTES):
    return pltpu.CompilerParams(dimension_semantics=semantics, vmem_limit_bytes=vmem_limit_bytes)


def _rms(x, g):
    ms = jnp.mean(x * x, axis=-1, keepdims=True)
    return x * lax.rsqrt(ms + RMS_EPS) * g


def _lane_group_sum(x):
    total = x[:, 0:LANES]
    for c in range(1, x.shape[1] // LANES):
        total = total + x[:, c * LANES:(c + 1) * LANES]
    return total


def _row_sumsq(h):
    part = _lane_group_sum(h * h)
    return jnp.broadcast_to(jnp.sum(part, axis=-1, keepdims=True), part.shape)


def _row_scale(sumsq_ref, inv_d, width):
    r = lax.rsqrt(sumsq_ref[...] * inv_d + RMS_EPS)
    return jnp.concatenate([r] * (width // LANES), axis=1) if width > LANES else r


def _rmsnorm_kernel(x_ref, g_ref, o_ref):
    o_ref[...] = _rms(x_ref[...], g_ref[...]).astype(o_ref.dtype)


def _rmsnorm(x, g, out_dtype):
    n, d = x.shape
    tm = _tile(n, 256)
    return pl.pallas_call(
        _rmsnorm_kernel,
        grid=(n // tm,),
        in_specs=[pl.BlockSpec((tm, d), lambda i: (i, 0)),
                  pl.BlockSpec((1, d), lambda i: (0, 0))],
        out_specs=pl.BlockSpec((tm, d), lambda i: (i, 0)),
        out_shape=jax.ShapeDtypeStruct((n, d), out_dtype),
        compiler_params=_params(("parallel",)),
        name="rmsnorm",
    )(x, g.reshape(1, d))


def _ep_relu2(acc):
    return jnp.square(jnp.maximum(acc, 0.0))


def _ep_identity(acc):
    return acc


def _ep_residual(acc, res):
    return res + acc


def _ep_gate(acc, res, e):
    return res + jax.nn.sigmoid(acc) * e


def _matmul_kernel(*refs, nk, n_extra, epilogue, on_residual, inv_d_in, col_scaled, inv_d_ple,
                   emit_norm, n_casts):
    refs = list(refs)
    a_ref, w_ref = refs[:2]
    pos = 2
    sumsq_in_ref = col_scale_ref = None
    if inv_d_in is not None:
        sumsq_in_ref = refs[pos]
        pos += 1
    if col_scaled:
        col_scale_ref = refs[pos]
        pos += 1
    extras = refs[pos:pos + n_extra]
    pos += n_extra
    ple_refs = None
    if inv_d_ple is not None:
        ple_refs = refs[pos:pos + 4]
        pos += 4
    gain_ref = hg_ref = sumsq_ref = None
    if emit_norm:
        gain_ref = refs[pos]
        pos += 1
    cast_srcs = refs[pos:pos + n_casts]
    pos += n_casts
    o_ref = refs[pos]
    pos += 1
    if emit_norm:
        hg_ref, sumsq_ref = refs[pos:pos + 2]
        pos += 2
    cast_dsts = refs[pos:pos + n_casts]
    pos += n_casts
    j = pl.program_id(1)

    def partial_product():
        for src, dst in zip(cast_srcs, cast_dsts):
            dst[...] = src[...].astype(dst.dtype)
        return jnp.dot(a_ref[...], w_ref[...], preferred_element_type=_F32)

    def emit(h_new):
        hg_ref[...] = (h_new * gain_ref[...]).astype(hg_ref.dtype)
        tile_sumsq = _row_sumsq(h_new)

        @pl.when(j == 0)
        def _():
            sumsq_ref[...] = tile_sumsq

        @pl.when(j > 0)
        def _():
            sumsq_ref[...] += tile_sumsq

    def embedding_tile():
        p_ref, w_pe_ref, g_pe_ref, ple_sumsq_ref = ple_refs
        y = jnp.dot(p_ref[...].astype(_BF16), w_pe_ref[...], preferred_element_type=_F32)
        return y * _row_scale(ple_sumsq_ref, inv_d_ple, y.shape[1]) * g_pe_ref[...]

    def finish(acc, *more):
        if sumsq_in_ref is not None:
            acc = acc * _row_scale(sumsq_in_ref, inv_d_in, acc.shape[1])
        if col_scale_ref is not None:
            acc = acc * col_scale_ref[...]
        out = epilogue(acc, *[e[...] for e in extras], *more)
        o_ref[...] = out.astype(o_ref.dtype)
        if emit_norm:
            emit(out)

    if nk == 1:
        more = (embedding_tile(),) if ple_refs is not None else ()
        finish(partial_product(), *more)
        return
    assert ple_refs is None
    k = pl.program_id(2)
    if on_residual:
        @pl.when(k == 0)
        def _():
            o_ref[...] = extras[0][...] + partial_product()

        @pl.when(jnp.logical_and(k > 0, k < nk - 1))
        def _():
            o_ref[...] += partial_product()

        @pl.when(k == nk - 1)
        def _():
            out = o_ref[...] + partial_product()
            o_ref[...] = out
            if emit_norm:
                emit(out)
        return
    acc_ref = refs[pos]

    @pl.when(k == 0)
    def _():
        acc_ref[...] = jnp.zeros_like(acc_ref)

    acc_ref[...] += partial_product()

    @pl.when(k == nk - 1)
    def _():
        finish(acc_ref[...])


def _cast_blocks(rows, n_steps):
    n_blocks = 1
    while (n_blocks * 2 <= n_steps and rows % (n_blocks * 2) == 0
           and rows // (n_blocks * 2) >= BF16_SUBLANES):
        n_blocks *= 2
    return rows // n_blocks, n_blocks


def _matmul(a, w, *, epilogue, out_dtype, tm, tn, tk, name,
            sumsq_in=None, col_scale=None, extras=(), ple=None, next_gain=None, casts=()):
    m, k = a.shape
    n = w.shape[-1]
    tm, tn, tk = _tile(m, tm), _tile(n, tn), _tile(k, tk)
    nj, nk = n // tn, k // tk
    n_steps = (m // tm) * nj * nk
    emit_norm = next_gain is not None
    on_residual = (nk > 1 and epilogue is _ep_residual and out_dtype == _F32
                   and sumsq_in is None and col_scale is None)
    tile_spec = pl.BlockSpec((tm, tn), lambda i, j, kk: (i, j))
    row_spec = pl.BlockSpec((tm, LANES), lambda i, j, kk: (i, 0))
    col_spec = pl.BlockSpec((1, tn), lambda i, j, kk: (0, j))
    in_specs = [pl.BlockSpec((tm, tk), lambda i, j, kk: (i, kk)),
                pl.BlockSpec((tk, tn), lambda i, j, kk: (kk, j))]
    operands = [a, w]
    if sumsq_in is not None:
        in_specs.append(row_spec)
        operands.append(sumsq_in)
    if col_scale is not None:
        in_specs.append(col_spec)
        operands.append(col_scale.reshape(1, n))
    in_specs += [tile_spec] * len(extras)
    operands += list(extras)
    if ple is not None:
        p, ple_layer, w_pe, g_pe, ple_sumsq = ple
        pd = p.shape[-1]
        in_specs += [pl.BlockSpec((None, tm, pd), lambda i, j, kk: (ple_layer, i, 0)),
                     pl.BlockSpec((None, pd, tn), lambda i, j, kk: (ple_layer, 0, j)),
                     col_spec, row_spec]
        operands += [p, w_pe, g_pe.reshape(1, n), ple_sumsq]
    out_specs = [tile_spec]
    out_shape = [jax.ShapeDtypeStruct((m, n), out_dtype)]
    if emit_norm:
        in_specs.append(col_spec)
        operands.append(next_gain.reshape(1, n))
        out_specs += [tile_spec, row_spec]
        out_shape += [jax.ShapeDtypeStruct((m, n), _BF16), jax.ShapeDtypeStruct((m, LANES), _F32)]
    for src, src_layer in casts:
        _, rows, cols = src.shape
        rb, n_blocks = _cast_blocks(rows, n_steps)

        def block_of(i, j, kk, n_blocks=n_blocks):
            return (((i * nj + j) * nk + kk) * n_blocks) // n_steps

        in_specs.append(pl.BlockSpec(
            (None, rb, cols), lambda i, j, kk, f=block_of, l=src_layer: (l, f(i, j, kk), 0)))
        operands.append(src)
        out_specs.append(pl.BlockSpec((rb, cols), lambda i, j, kk, f=block_of: (f(i, j, kk), 0)))
        out_shape.append(jax.ShapeDtypeStruct((rows, cols), _BF16))
    kernel = functools.partial(
        _matmul_kernel, nk=nk, n_extra=len(extras), epilogue=epilogue, on_residual=on_residual,
        inv_d_in=None if sumsq_in is None else 1.0 / k, col_scaled=col_scale is not None,
        inv_d_ple=None if ple is None else 1.0 / n, emit_norm=emit_norm, n_casts=len(casts))
    outs = pl.pallas_call(
        kernel,
        grid=(m // tm, nj, nk),
        in_specs=in_specs,
        out_specs=out_specs,
        out_shape=out_shape,
        scratch_shapes=[pltpu.VMEM((tm, tn), _F32)] if nk > 1 and not on_residual else [],
        compiler_params=_params(("arbitrary", "arbitrary", "arbitrary"),
                                V7X_VMEM_BYTES - 2 * 1024 * 1024 if nk == 4 else VMEM_LIMIT_BYTES),
        name=name,
    )(*operands)
    return outs[0] if len(outs) == 1 else tuple(outs)


def _pool_kernel(h_ref, halo_ref, g_ref, w_ref, scale_ref, gain_ref, o_ref, hg_ref, sumsq_ref,
                 xn_ref, lvl_a_ref, lvl_b_ref, *, ts, dg):
    i = pl.program_id(1)
    g = g_ref[...]
    h = h_ref[0]
    first = SUBLANES
    tile0 = first + POOL_HALO
    end = tile0 + ts
    xn_ref[0:first, :] = jnp.zeros((first, xn_ref.shape[1]), _F32)
    xn_ref[first:tile0, :] = jnp.where(i > 0, _rms(halo_ref[0], g), 0.0)
    xn_ref[tile0:, :] = _rms(h, g)
    for lvl_ref in (lvl_a_ref, lvl_b_ref):
        lvl_ref[0:first, :] = jnp.zeros((first, dg), _F32)
    pos = i * ts + lax.broadcasted_iota(jnp.int32, (ts, 1), 0)
    sumsq = None
    for gi, win in enumerate(POOL_WINDOWS):
        cols = slice(gi * dg, (gi + 1) * dg)
        cur = xn_ref[tile0:end, cols]
        src = lambda lo, hi, cols=cols: xn_ref[lo:hi, cols]
        width, levels = 1, [lvl_a_ref, lvl_b_ref]
        while 2 * width < win:
            dst = levels[0]
            dst[first:end, :] = src(first, end) + src(first - width, end - width)
            src = lambda lo, hi, dst=dst: dst[lo:hi, :]
            levels.reverse()
            width *= 2
        tot = src(tile0, end) + src(tile0 - width, end - width)
        cnt = jnp.minimum(pos + 1, win).astype(_F32)
        pooled = tot / cnt - cur
        y = jnp.dot(pooled.astype(_BF16), w_ref[gi], preferred_element_type=_F32)
        h_new = h[:, cols] + y * scale_ref[:, cols]
        o_ref[0, :, cols] = h_new
        hg_ref[:, cols] = (h_new * gain_ref[:, cols]).astype(hg_ref.dtype)
        part = _lane_group_sum(h_new * h_new)
        sumsq = part if sumsq is None else sumsq + part
    sumsq_ref[...] = jnp.broadcast_to(jnp.sum(sumsq, axis=-1, keepdims=True), sumsq.shape)


def _pool_layer(h, g, w_pool, layer, scale, next_gain):
    b, s, d = h.shape
    ng = len(POOL_WINDOWS)
    dg = d // ng
    ts = _tile(s, 256)
    ns = s // ts
    halo_blocks = ts // POOL_HALO
    rows = SUBLANES + POOL_HALO + ts
    kernel = functools.partial(_pool_kernel, ts=ts, dg=dg)
    vec_spec = pl.BlockSpec((1, d), lambda bi, i: (0, 0))
    return pl.pallas_call(
        kernel,
        grid=(b, ns),
        in_specs=[
            pl.BlockSpec((1, ts, d), lambda bi, i: (bi, i, 0)),
            pl.BlockSpec((1, POOL_HALO, d),
                         lambda bi, i: (bi, jnp.maximum(i * halo_blocks - 1, 0), 0)),
            vec_spec,
            pl.BlockSpec((None, ng, dg, dg), lambda bi, i: (layer, 0, 0, 0)),
            vec_spec,
            vec_spec,
        ],
        out_specs=[pl.BlockSpec((1, ts, d), lambda bi, i: (bi, i, 0)),
                   pl.BlockSpec((ts, d), lambda bi, i: (bi * ns + i, 0)),
                   pl.BlockSpec((ts, LANES), lambda bi, i: (bi * ns + i, 0))],
        out_shape=[jax.ShapeDtypeStruct((b, s, d), _F32),
                   jax.ShapeDtypeStruct((b * s, d), _BF16),
                   jax.ShapeDtypeStruct((b * s, LANES), _F32)],
        scratch_shapes=[pltpu.VMEM((rows, d), _F32), pltpu.VMEM((rows, dg), _F32),
                        pltpu.VMEM((rows, dg), _F32)],
        compiler_params=_params(("parallel", "parallel")),
        name="pool_mixer",
    )(h, h, g.reshape(1, d), w_pool, scale.reshape(1, d), next_gain.reshape(1, d))


ATTN_HEADS_PER_STEP = 8
ATTN_Q_SCALE = _LOG2_E / math.sqrt(HEAD_DIM)


def _attn_kernel(q_ref, k_ref, v_ref, o_ref, carry_ref, acc_ref, *, tq, nhb):
    qi = pl.program_id(2)
    row = lax.broadcasted_iota(jnp.int32, (tq, tq), 0)
    col = lax.broadcasted_iota(jnp.int32, (tq, tq), 1)
    from_key = jnp.where(row >= col, 1.0, 0.0).astype(_BF16)
    from_key2 = jnp.concatenate([from_key, from_key], axis=0)
    causal = col < row
    heads = [slice(hh * HEAD_DIM, (hh + 1) * HEAD_DIM) for hh in range(nhb)]
    sign_bit = jnp.uint32(0x80000000)

    def step(starts, diagonal):
        zs = [[lax.dot_general(q_ref[:, cols], k_ref[pl.ds(st, tq), cols],
                               (((1,), (1,)), ((), ())), preferred_element_type=_F32)
               for cols in heads] for st in starts]
        tails = []
        for blk, per_head in enumerate(zs):
            tails.append([])
            for z in per_head:
                neg_abs = lax.bitcast_convert_type(
                    lax.bitcast_convert_type(z, jnp.uint32) | sign_bit, _F32)
                sp = jnp.maximum(z, 0.0) + jnp.log(1.0 + jnp.exp2(neg_abs)) * _LOG2_E
                if diagonal and blk == 0:
                    sp = jnp.where(causal, sp, 0.0)
                hi = sp.astype(_BF16)
                lo = (sp - hi.astype(_F32)).astype(_BF16)
                tails[blk].append(jnp.dot(jnp.concatenate([hi, lo], axis=1), from_key2,
                                          preferred_element_type=_F32))
        for hh, cols in enumerate(heads):
            carry = jnp.zeros((tq, LANES), _F32) if diagonal else carry_ref[hh]
            weights = []
            for blk in range(len(starts)):
                tail = tails[blk][hh]
                attn = jnp.exp2(zs[blk][hh] - jnp.concatenate([carry] * (tq // LANES), axis=1) - tail)
                if diagonal and blk == 0:
                    attn = jnp.where(causal, attn, 0.0)
                weights.append(attn.astype(_BF16))
                carry = carry + jnp.broadcast_to(tail[:, 0:1], carry.shape)
            carry_ref[hh] = carry
            attn_all = weights[0] if len(weights) == 1 else jnp.concatenate(weights[::-1], axis=1)
            av = jnp.dot(attn_all, v_ref[pl.ds(starts[-1], tq * len(starts)), cols],
                         preferred_element_type=_F32)
            acc_ref[hh] = av if diagonal else acc_ref[hh] + av

    def block_start(kb):
        return pl.multiple_of(kb * tq, tq)

    step([block_start(qi)], True)
    odd = qi % 2

    @pl.when(odd == 1)
    def _():
        step([block_start(qi - 1)], False)

    def body(j, _):
        later = qi - 1 - odd - 2 * j
        step([block_start(later), block_start(later - 1)], False)
        return 0

    lax.fori_loop(0, (qi - odd) // 2, body, 0)
    for hh, cols in enumerate(heads):
        o_ref[:, cols] = acc_ref[hh].astype(o_ref.dtype)


def _attention(qkv, b, s, d):
    nh = d // HEAD_DIM
    nhb = _tile(nh, ATTN_HEADS_PER_STEP)
    ng = nh // nhb
    tq = _tile(s, 256)
    nq = s // tq
    kernel = functools.partial(_attn_kernel, tq=tq, nhb=nhb)
    wide = nhb * HEAD_DIM
    return pl.pallas_call(
        kernel,
        grid=(b, ng, nq),
        in_specs=[
            pl.BlockSpec((tq, wide), lambda bi, gi, qi: (bi * nq + qi, gi)),
            pl.BlockSpec((s, wide), lambda bi, gi, qi: (bi, ng + gi)),
            pl.BlockSpec((s, wide), lambda bi, gi, qi: (bi, 2 * ng + gi)),
        ],
        out_specs=pl.BlockSpec((tq, wide), lambda bi, gi, qi: (bi * nq + qi, gi)),
        out_shape=jax.ShapeDtypeStruct((b * s, d), _BF16),
        scratch_shapes=[pltpu.VMEM((nhb, tq, LANES), _F32), pltpu.VMEM((nhb, tq, HEAD_DIM), _F32)],
        compiler_params=_params(("parallel", "parallel", "arbitrary")),
        name="stickbreak_attn",
    )(qkv, qkv, qkv)


def _ple_sumsq_kernel(p_ref, w_ref, o_ref):
    y = jnp.dot(p_ref[...].astype(_BF16), w_ref[...], preferred_element_type=_F32)
    o_ref[...] = _row_sumsq(y)


def _ple_sumsq(p, layer, w_pe):
    _, n, pd = p.shape
    d = w_pe.shape[-1]
    tm = _tile(n, 512)
    return pl.pallas_call(
        _ple_sumsq_kernel,
        grid=(n // tm,),
        in_specs=[pl.BlockSpec((None, tm, pd), lambda i: (layer, i, 0)),
                  pl.BlockSpec((None, pd, d), lambda i: (layer, 0, 0))],
        out_specs=pl.BlockSpec((tm, LANES), lambda i: (i, 0)),
        out_shape=jax.ShapeDtypeStruct((n, LANES), _F32),
        compiler_params=_params(("parallel",)),
        name="ple_sumsq",
    )(p, w_pe)


def kernel(x, p, g_mix, g_mlp, w_pool, pool_scale, w_qkv, w_o, w_up, w_down,
           w_pe, g_pe, g_gate, w_gate, g_final):
    b, s, d = x.shape
    depth = g_mix.shape[0]
    n = b * s
    p2 = p.reshape(depth, n, p.shape[-1])
    w_pool_b, w_pe_b = w_pool.astype(_BF16), w_pe.astype(_BF16)
    w_up_b = w_up[0].astype(_BF16)
    w_qkv_b = w_o_b = None
    q_scale = jnp.concatenate([jnp.full((d,), ATTN_Q_SCALE, _F32), jnp.ones((2 * d,), _F32)])

    h = x.reshape(n, d)
    hg = sumsq = None
    for i in range(depth):
        if i % 2 == 0:
            h, hg, sumsq = _pool_layer(h.reshape(b, s, d), g_mix[i], w_pool_b, i // 2,
                                       pool_scale[i // 2], g_mlp[i])
            h = h.reshape(n, d)
        else:
            qkv, w_up_b = _matmul(hg, w_qkv_b, sumsq_in=sumsq, col_scale=q_scale,
                                  epilogue=_ep_identity, out_dtype=_BF16, casts=[(w_up, i)],
                                  tm=1024, tn=1024, tk=4096, name="qkv_proj")
            o = _attention(qkv, b, s, d)
            h, hg, sumsq = _matmul(o, w_o_b, extras=(h,), epilogue=_ep_residual,
                                   out_dtype=_F32, next_gain=g_mlp[i],
                                   tm=1024, tn=512, tk=4096, name="attn_out_proj")
        feeds_attention = i + 1 < depth and (i + 1) % 2 == 1
        feeds_pool = i + 1 < depth and not feeds_attention
        casts = [(w_down, i), (w_gate, i)]
        if feeds_attention:
            casts += [(w_qkv, (i + 1) // 2), (w_o, (i + 1) // 2)]
        outs = _matmul(hg, w_up_b, sumsq_in=sumsq, epilogue=_ep_relu2, out_dtype=_BF16,
                       casts=casts, tm=1024, tn=1024, tk=4096, name="mlp_up")
        a, w_down_b, w_gate_b = outs[:3]
        if feeds_attention:
            w_qkv_b, w_o_b = outs[3:]
        h, hg, sumsq = _matmul(a, w_down_b, extras=(h,), epilogue=_ep_residual, out_dtype=_F32,
                               next_gain=g_gate[i], tm=1024, tn=1024, tk=4096, name="mlp_down")
        ple = (p2, i, w_pe_b, g_pe[i], _ple_sumsq(p2, i, w_pe_b))
        outs = _matmul(hg, w_gate_b, sumsq_in=sumsq, extras=(h,), ple=ple, epilogue=_ep_gate,
                       out_dtype=_F32, next_gain=g_mix[i + 1] if feeds_attention else None,
                       casts=[(w_up, i + 1)] if feeds_pool else [],
                       tm=1024, tn=512, tk=4096, name="ple_gate")
        hg = sumsq = None
        if feeds_attention:
            h, hg, sumsq = outs
        elif feeds_pool:
            h, w_up_b = outs
        else:
            h = outs
    return _rmsnorm(h, g_final, _F32).reshape(b, s, d)
```
